```python
import math
import jax, jax.numpy as jnp
from jax import lax
import numpy as np

D_MODEL = 1024
BATCH = 8
SEQ = 8192
DEPTH = 4

N_A = DEPTH // 2
N_B = DEPTH - N_A
GDN_HEADS = 8
GDN_HEAD_DIM = D_MODEL // GDN_HEADS
GDN_CONV = 4
GDN_CHUNK = 64
SB_HEADS = 8
SB_HEAD_DIM = D_MODEL // SB_HEADS
SB_BLOCK = 128
D_FF = ((8 * D_MODEL // 3 + 127) // 128) * 128
FFN_CONV = 3
EPS = 1e-6

kernel_name = "yoco_gdn_stickbreaking_convffn"


def rmsnorm(x, g):
    xf = x.astype(jnp.float32)
    xf = xf * lax.rsqrt(jnp.mean(xf * xf, axis=-1, keepdims=True) + EPS)
    return (xf * g.astype(jnp.float32)).astype(x.dtype)


def l2norm(x):
    xf = x.astype(jnp.float32)
    return xf * lax.rsqrt(jnp.sum(xf * xf, axis=-1, keepdims=True) + EPS)


def causal_dwconv(x, w):
    K, C = w.shape
    return lax.conv_general_dilated(
        x, w[:, None, :].astype(x.dtype), window_strides=(1,), padding=[(K - 1, 0)],
        dimension_numbers=("NWC", "WIO", "NWC"), feature_group_count=C)


def gated_delta_rule(q, k, v, g, beta):
    B, T, H, Dk = q.shape
    Dv = v.shape[-1]
    C = GDN_CHUNK
    N = T // C
    f32 = jnp.float32

    def chunks(t):
        t = t.astype(f32).reshape((B, N, C, H) + t.shape[3:])
        return jnp.moveaxis(t, 3, 1)

    q, k, v, beta = chunks(q), chunks(k), chunks(v), chunks(beta)
    g = jnp.cumsum(chunks(g), axis=-1)
    tri = jnp.tril(jnp.ones((C, C), dtype=bool))
    strict = jnp.tril(jnp.ones((C, C), dtype=bool), -1)
    decay = jnp.exp(jnp.where(tri, g[..., :, None] - g[..., None, :], -jnp.inf))
    k_beta = k * beta[..., None]
    L = jnp.where(strict, jnp.einsum("bhnid,bhnjd->bhnij", k_beta, k) * decay, 0.0)
    rhs = jnp.concatenate([v * beta[..., None], k_beta * jnp.exp(g)[..., None]], axis=-1)
    sol = lax.linalg.triangular_solve(L, rhs, left_side=True, lower=True, unit_diagonal=True)
    u, w = sol[..., :Dv], sol[..., Dv:]
    qk = jnp.where(tri, jnp.einsum("bhnid,bhnjd->bhnij", q, k) * decay, 0.0)
    g_last = g[..., -1]
    q_dec = q * jnp.exp(g)[..., None]
    k_dec = k * jnp.exp(g_last[..., None] - g)[..., None]

    def step(S, inp):
        qk_i, q_dec_i, k_dec_i, u_i, w_i, gl_i = inp
        v_new = u_i - jnp.einsum("bhck,bhkv->bhcv", w_i, S)
        o = jnp.einsum("bhck,bhkv->bhcv", q_dec_i, S) + jnp.einsum("bhij,bhjv->bhiv", qk_i, v_new)
        S = S * jnp.exp(gl_i)[..., None, None] + jnp.einsum("bhck,bhcv->bhkv", k_dec_i, v_new)
        return S, o

    xs = tuple(jnp.moveaxis(t, 2, 0) for t in (qk, q_dec, k_dec, u, w, g_last))
    S0 = jnp.zeros((B, H, Dk, Dv), f32)
    _, o = lax.scan(step, S0, xs)
    return jnp.transpose(o, (1, 0, 3, 2, 4)).reshape(B, T, H, Dv)


def gdn_mixer(h, w_in, conv_w, a_log, dt_bias, o_gain, w_out):
    B, T, D = h.shape
    H, Dh = GDN_HEADS, GDN_HEAD_DIM
    proj = h @ w_in
    qkv, z, a, b = jnp.split(proj, [3 * D, 4 * D, 4 * D + H], axis=-1)
    qkv = jax.nn.silu(causal_dwconv(qkv, conv_w))
    q, k, v = jnp.split(qkv, 3, axis=-1)
    q = l2norm(q.reshape(B, T, H, Dh)) * (Dh ** -0.5)
    k = l2norm(k.reshape(B, T, H, Dh))
    v = v.reshape(B, T, H, Dh)
    g = -jnp.exp(a_log.astype(jnp.float32)) * jax.nn.softplus(
        a.astype(jnp.float32) + dt_bias.astype(jnp.float32))
    beta = jax.nn.sigmoid(b.astype(jnp.float32))
    o = gated_delta_rule(q, k, v, g, beta)
    o = rmsnorm(o, o_gain) * jax.nn.silu(z.reshape(B, T, H, Dh).astype(jnp.float32))
    return o.reshape(B, T, D).astype(h.dtype) @ w_out


def stick_breaking_attention(q, k, v):
    B, H, T, D = q.shape
    scale = D ** -0.5
    outs = []
    for blk in range(T // SB_BLOCK):
        lo, hi = blk * SB_BLOCK, (blk + 1) * SB_BLOCK
        z = jnp.einsum("bhtd,bhsd->bhts", q[:, :, lo:hi], k[:, :, :hi],
                       preferred_element_type=jnp.float32) * scale
        t_idx = lo + jnp.arange(SB_BLOCK)[:, None]
        s_idx = jnp.arange(hi)[None, :]
        causal = s_idx < t_idx
        log_fail = jnp.where(causal, jax.nn.log_sigmoid(-z), 0.0)
        after = lax.cumsum(log_fail, axis=3, reverse=True) - log_fail
        a = jnp.where(causal, jnp.exp(jax.nn.log_sigmoid(z) + after), 0.0)
        outs.append(jnp.einsum("bhts,bhsd->bhtd", a, v[:, :, :hi].astype(jnp.float32)))
    return jnp.concatenate(outs, axis=2).astype(q.dtype)


def conv_ffn(h, w_up, conv_w, w_down):
    u = causal_dwconv(h @ w_up, conv_w)
    gate, up = jnp.split(u, 2, axis=-1)
    return (jax.nn.silu(gate) * up) @ w_down


def _fwd_setup_inputs(seed: int = 0) -> dict:
    key = jax.random.key(seed)
    ks = jax.random.split(key, 20)
    D, H = D_MODEL, GDN_HEADS
    f32 = jnp.float32
    out_scale = (2 * DEPTH) ** -0.5

    def nrm(k, shape, fan_in, gain=1.0):
        return jax.random.normal(k, shape, f32) * (gain * fan_in ** -0.5)

    def gvec(k, shape):
        return 1.0 + 0.02 * jax.random.normal(k, shape, f32)

    dt = jnp.exp(jax.random.uniform(ks[4], (N_A, H), f32, math.log(1e-3), math.log(1e-1)))
    return {
        "x": jax.random.normal(ks[0], (BATCH, SEQ, D), f32),
        "a_norm": gvec(ks[1], (N_A, D)),
        "a_w_in": nrm(ks[2], (N_A, D, 4 * D + 2 * H), D),
        "a_conv": nrm(ks[3], (N_A, GDN_CONV, 3 * D), GDN_CONV),
        "a_log": jnp.log(jax.random.uniform(ks[5], (N_A, H), f32, 1.0, 16.0)),
        "a_dt_bias": dt + jnp.log(-jnp.expm1(-dt)),
        "a_out_norm": gvec(ks[6], (N_A, GDN_HEAD_DIM)),
        "a_w_out": nrm(ks[7], (N_A, D, D), D, out_scale),
        "kv_norm": gvec(ks[8], (D,)),
        "w_kv": nrm(ks[9], (D, 2 * D), D),
        "k_norm": gvec(ks[10], (SB_HEAD_DIM,)),
        "b_norm": gvec(ks[11], (N_B, D)),
        "b_w_q": nrm(ks[12], (N_B, D, D), D),
        "q_norm": gvec(ks[13], (N_B, SB_HEAD_DIM)),
        "b_w_out": nrm(ks[14], (N_B, D, D), D, out_scale),
        "ffn_norm": gvec(ks[15], (DEPTH, D)),
        "ffn_w_up": nrm(ks[16], (DEPTH, D, 2 * D_FF), D),
        "ffn_conv": nrm(ks[17], (DEPTH, FFN_CONV, 2 * D_FF), FFN_CONV),
        "ffn_w_down": nrm(ks[18], (DEPTH, D_FF, D), D_FF, out_scale),
    }


def _fwd_reference(x, a_norm, a_w_in, a_conv, a_log, a_dt_bias, a_out_norm, a_w_out,
              kv_norm, w_kv, k_norm, b_norm, b_w_q, q_norm, b_w_out,
              ffn_norm, ffn_w_up, ffn_conv, ffn_w_down):
    B, T, D = x.shape
    H, Dh = SB_HEADS, SB_HEAD_DIM
    for layer in range(DEPTH):
        if layer < N_A:
            x = x + gdn_mixer(rmsnorm(x, a_norm[layer]), a_w_in[layer], a_conv[layer],
                              a_log[layer], a_dt_bias[layer], a_out_norm[layer], a_w_out[layer])
        else:
            j = layer - N_A
            if j == 0:
                kv = rmsnorm(x, kv_norm) @ w_kv
                k_s, v_s = jnp.split(kv, 2, axis=-1)
                k_s = jnp.transpose(rmsnorm(k_s.reshape(B, T, H, Dh), k_norm), (0, 2, 1, 3))
                v_s = jnp.transpose(v_s.reshape(B, T, H, Dh), (0, 2, 1, 3))
            q = rmsnorm((rmsnorm(x, b_norm[j]) @ b_w_q[j]).reshape(B, T, H, Dh), q_norm[j])
            o = stick_breaking_attention(jnp.transpose(q, (0, 2, 1, 3)), k_s, v_s)
            x = x + jnp.transpose(o, (0, 2, 1, 3)).reshape(B, T, D) @ b_w_out[j]
        x = x + conv_ffn(rmsnorm(x, ffn_norm[layer]), ffn_w_up[layer], ffn_conv[layer], ffn_w_down[layer])
    return x


import jax as _jax
import jax.numpy as _jnp

TWIN_FORMAT = 'train_step'
FWD_PARAMS = ['x', 'a_norm', 'a_w_in', 'a_conv', 'a_log', 'a_dt_bias', 'a_out_norm', 'a_w_out', 'kv_norm', 'w_kv', 'k_norm', 'b_norm', 'b_w_q', 'q_norm', 'b_w_out', 'ffn_norm', 'ffn_w_up', 'ffn_conv', 'ffn_w_down']
TWIN_WEIGHTS = ['a_norm', 'a_w_in', 'a_conv', 'a_log', 'a_dt_bias', 'a_out_norm', 'a_w_out', 'kv_norm', 'w_kv', 'k_norm', 'b_norm', 'b_w_q', 'q_norm', 'b_w_out', 'ffn_norm', 'ffn_w_up', 'ffn_conv', 'ffn_w_down']
TWIN_DIFF_INPUT = 'x'
TWIN_INPUTS = ['x', 'a_norm', 'a_w_in', 'a_conv', 'a_log', 'a_dt_bias', 'a_out_norm', 'a_w_out', 'kv_norm', 'w_kv', 'k_norm', 'b_norm', 'b_w_q', 'q_norm', 'b_w_out', 'ffn_norm', 'ffn_w_up', 'ffn_conv', 'ffn_w_down', 'loss_target', 'm_a_norm', 'm_a_w_in', 'm_a_conv', 'm_a_log', 'm_a_dt_bias', 'm_a_out_norm', 'm_a_w_out', 'm_kv_norm', 'm_w_kv', 'm_k_norm', 'm_b_norm', 'm_b_w_q', 'm_q_norm', 'm_b_w_out', 'm_ffn_norm', 'm_ffn_w_up', 'm_ffn_conv', 'm_ffn_w_down', 'v_a_norm', 'v_a_w_in', 'v_a_conv', 'v_a_log', 'v_a_dt_bias', 'v_a_out_norm', 'v_a_w_out', 'v_kv_norm', 'v_w_kv', 'v_k_norm', 'v_b_norm', 'v_b_w_q', 'v_q_norm', 'v_b_w_out', 'v_ffn_norm', 'v_ffn_w_up', 'v_ffn_conv', 'v_ffn_w_down']
TWIN_OUTPUTS = ['loss', 'grad_x', 'grad_a_norm', 'grad_a_w_in', 'grad_a_conv', 'grad_a_log', 'grad_a_dt_bias', 'grad_a_out_norm', 'grad_a_w_out', 'grad_kv_norm', 'grad_w_kv', 'grad_k_norm', 'grad_b_norm', 'grad_b_w_q', 'grad_q_norm', 'grad_b_w_out', 'grad_ffn_norm', 'grad_ffn_w_up', 'grad_ffn_conv', 'grad_ffn_w_down', 'delta_a_norm', 'delta_a_w_in', 'delta_a_conv', 'delta_a_log', 'delta_a_dt_bias', 'delta_a_out_norm', 'delta_a_w_out', 'delta_kv_norm', 'delta_w_kv', 'delta_k_norm', 'delta_b_norm', 'delta_b_w_q', 'delta_q_norm', 'delta_b_w_out', 'delta_ffn_norm', 'delta_ffn_w_up', 'delta_ffn_conv', 'delta_ffn_w_down', 'new_m_a_norm', 'new_m_a_w_in', 'new_m_a_conv', 'new_m_a_log', 'new_m_a_dt_bias', 'new_m_a_out_norm', 'new_m_a_w_out', 'new_m_kv_norm', 'new_m_w_kv', 'new_m_k_norm', 'new_m_b_norm', 'new_m_b_w_q', 'new_m_q_norm', 'new_m_b_w_out', 'new_m_ffn_norm', 'new_m_ffn_w_up', 'new_m_ffn_conv', 'new_m_ffn_w_down', 'new_v_a_norm', 'new_v_a_w_in', 'new_v_a_conv', 'new_v_a_log', 'new_v_a_dt_bias', 'new_v_a_out_norm', 'new_v_a_w_out', 'new_v_kv_norm', 'new_v_w_kv', 'new_v_k_norm', 'new_v_b_norm', 'new_v_b_w_q', 'new_v_q_norm', 'new_v_b_w_out', 'new_v_ffn_norm', 'new_v_ffn_w_up', 'new_v_ffn_conv', 'new_v_ffn_w_down']
TWIN_LEAF_KINDS = {'loss': 'loss', 'grad_x': 'grad_x', 'grad_a_norm': 'grad_w', 'grad_a_w_in': 'grad_w', 'grad_a_conv': 'grad_w', 'grad_a_log': 'grad_w', 'grad_a_dt_bias': 'grad_w', 'grad_a_out_norm': 'grad_w', 'grad_a_w_out': 'grad_w', 'grad_kv_norm': 'grad_w', 'grad_w_kv': 'grad_w', 'grad_k_norm': 'grad_w', 'grad_b_norm': 'grad_w', 'grad_b_w_q': 'grad_w', 'grad_q_norm': 'grad_w', 'grad_b_w_out': 'grad_w', 'grad_ffn_norm': 'grad_w', 'grad_ffn_w_up': 'grad_w', 'grad_ffn_conv': 'grad_w', 'grad_ffn_w_down': 'grad_w', 'delta_a_norm': 'delta_w', 'delta_a_w_in': 'delta_w', 'delta_a_conv': 'delta_w', 'delta_a_log': 'delta_w', 'delta_a_dt_bias': 'delta_w', 'delta_a_out_norm': 'delta_w', 'delta_a_w_out': 'delta_w', 'delta_kv_norm': 'delta_w', 'delta_w_kv': 'delta_w', 'delta_k_norm': 'delta_w', 'delta_b_norm': 'delta_w', 'delta_b_w_q': 'delta_w', 'delta_q_norm': 'delta_w', 'delta_b_w_out': 'delta_w', 'delta_ffn_norm': 'delta_w', 'delta_ffn_w_up': 'delta_w', 'delta_ffn_conv': 'delta_w', 'delta_ffn_w_down': 'delta_w', 'new_m_a_norm': 'new_m', 'new_m_a_w_in': 'new_m', 'new_m_a_conv': 'new_m', 'new_m_a_log': 'new_m', 'new_m_a_dt_bias': 'new_m', 'new_m_a_out_norm': 'new_m', 'new_m_a_w_out': 'new_m', 'new_m_kv_norm': 'new_m', 'new_m_w_kv': 'new_m', 'new_m_k_norm': 'new_m', 'new_m_b_norm': 'new_m', 'new_m_b_w_q': 'new_m', 'new_m_q_norm': 'new_m', 'new_m_b_w_out': 'new_m', 'new_m_ffn_norm': 'new_m', 'new_m_ffn_w_up': 'new_m', 'new_m_ffn_conv': 'new_m', 'new_m_ffn_w_down': 'new_m', 'new_v_a_norm': 'new_v', 'new_v_a_w_in': 'new_v', 'new_v_a_conv': 'new_v', 'new_v_a_log': 'new_v', 'new_v_a_dt_bias': 'new_v', 'new_v_a_out_norm': 'new_v', 'new_v_a_w_out': 'new_v', 'new_v_kv_norm': 'new_v', 'new_v_w_kv': 'new_v', 'new_v_k_norm': 'new_v', 'new_v_b_norm': 'new_v', 'new_v_b_w_q': 'new_v', 'new_v_q_norm': 'new_v', 'new_v_b_w_out': 'new_v', 'new_v_ffn_norm': 'new_v', 'new_v_ffn_w_up': 'new_v', 'new_v_ffn_conv': 'new_v', 'new_v_ffn_w_down': 'new_v'}


def _forward(args):
    return _fwd_reference(*[args[k] for k in FWD_PARAMS])


def _output_shape():
    def fwd():
        inp = _fwd_setup_inputs(0)
        return _fwd_reference(*[inp[k] for k in FWD_PARAMS])
    out = _jax.eval_shape(fwd)
    return out.shape, out.dtype

N_MICROBATCH = 1
ADAM_LR = 0.001
ADAM_B1 = 0.9
ADAM_B2 = 0.999
ADAM_EPS = 1e-08
ADAM_WD = 0.01
ADAM_STEP = 10
PER_EXAMPLE_BATCH_AXIS = {'x': 0, 'loss_target': 0}
SHARED_INPUTS = []
_WEIGHT_DTYPES = {'a_norm': _jnp.float32, 'a_w_in': _jnp.float32, 'a_conv': _jnp.float32, 'a_log': _jnp.float32, 'a_dt_bias': _jnp.float32, 'a_out_norm': _jnp.float32, 'a_w_out': _jnp.float32, 'kv_norm': _jnp.float32, 'w_kv': _jnp.float32, 'k_norm': _jnp.float32, 'b_norm': _jnp.float32, 'b_w_q': _jnp.float32, 'q_norm': _jnp.float32, 'b_w_out': _jnp.float32, 'ffn_norm': _jnp.float32, 'ffn_w_up': _jnp.float32, 'ffn_conv': _jnp.float32, 'ffn_w_down': _jnp.float32}
MOMENT_SCALE = {'a_norm': 3.275728e+00, 'a_w_in': 1.251339e-01, 'a_conv': 1.555152e-01, 'a_log': 9.359528e+00, 'a_dt_bias': 8.925711e+00, 'a_out_norm': 2.312464e+01, 'a_w_out': 8.994306e-01, 'kv_norm': 6.637926e+00, 'w_kv': 1.769989e-01, 'k_norm': 7.813153e+00, 'b_norm': 5.924790e-02, 'b_w_q': 5.727543e-02, 'q_norm': 3.856822e+00, 'b_w_out': 4.519022e-01, 'ffn_norm': 6.363775e+00, 'ffn_w_up': 8.517383e-02, 'ffn_conv': 8.584897e-01, 'ffn_w_down': 3.616087e-01}


def _to_microbatches(a, axis):
    t = _jnp.moveaxis(a, axis, 0)
    t = t.reshape((N_MICROBATCH, t.shape[0] // N_MICROBATCH) + t.shape[1:])
    return _jnp.moveaxis(t, 1, axis + 1)


def setup_inputs(seed: int = 0) -> dict:
    inp = _fwd_setup_inputs(seed)
    key = _jax.random.fold_in(_jax.random.key(seed), 7919)
    shape, _ = _output_shape()
    out = dict(inp)
    out["loss_target"] = _jax.random.normal(_jax.random.fold_in(key, 0), shape, _jnp.float32)
    for i, name in enumerate(TWIN_WEIGHTS):
        w = inp[name].astype(_jnp.float32)
        if MOMENT_SCALE is None:
            s = _jnp.sqrt(_jnp.mean(_jnp.square(w)) + 1e-30)
        else:
            s = MOMENT_SCALE[name]
        km, kv = _jax.random.split(_jax.random.fold_in(key, i + 1))
        out[name] = w
        out["m_" + name] = s * _jax.random.normal(km, w.shape, _jnp.float32)
        out["v_" + name] = (s * s) * _jax.random.uniform(kv, w.shape, _jnp.float32, 0.5, 1.5)
    if N_MICROBATCH > 1:
        for name, axis in PER_EXAMPLE_BATCH_AXIS.items():
            out[name] = _to_microbatches(out[name], axis)
    return {'x': out['x'], 'a_norm': out['a_norm'], 'a_w_in': out['a_w_in'], 'a_conv': out['a_conv'], 'a_log': out['a_log'], 'a_dt_bias': out['a_dt_bias'], 'a_out_norm': out['a_out_norm'], 'a_w_out': out['a_w_out'], 'kv_norm': out['kv_norm'], 'w_kv': out['w_kv'], 'k_norm': out['k_norm'], 'b_norm': out['b_norm'], 'b_w_q': out['b_w_q'], 'q_norm': out['q_norm'], 'b_w_out': out['b_w_out'], 'ffn_norm': out['ffn_norm'], 'ffn_w_up': out['ffn_w_up'], 'ffn_conv': out['ffn_conv'], 'ffn_w_down': out['ffn_w_down'], 'loss_target': out['loss_target'], 'm_a_norm': out['m_a_norm'], 'm_a_w_in': out['m_a_w_in'], 'm_a_conv': out['m_a_conv'], 'm_a_log': out['m_a_log'], 'm_a_dt_bias': out['m_a_dt_bias'], 'm_a_out_norm': out['m_a_out_norm'], 'm_a_w_out': out['m_a_w_out'], 'm_kv_norm': out['m_kv_norm'], 'm_w_kv': out['m_w_kv'], 'm_k_norm': out['m_k_norm'], 'm_b_norm': out['m_b_norm'], 'm_b_w_q': out['m_b_w_q'], 'm_q_norm': out['m_q_norm'], 'm_b_w_out': out['m_b_w_out'], 'm_ffn_norm': out['m_ffn_norm'], 'm_ffn_w_up': out['m_ffn_w_up'], 'm_ffn_conv': out['m_ffn_conv'], 'm_ffn_w_down': out['m_ffn_w_down'], 'v_a_norm': out['v_a_norm'], 'v_a_w_in': out['v_a_w_in'], 'v_a_conv': out['v_a_conv'], 'v_a_log': out['v_a_log'], 'v_a_dt_bias': out['v_a_dt_bias'], 'v_a_out_norm': out['v_a_out_norm'], 'v_a_w_out': out['v_a_w_out'], 'v_kv_norm': out['v_kv_norm'], 'v_w_kv': out['v_w_kv'], 'v_k_norm': out['v_k_norm'], 'v_b_norm': out['v_b_norm'], 'v_b_w_q': out['v_b_w_q'], 'v_q_norm': out['v_q_norm'], 'v_b_w_out': out['v_b_w_out'], 'v_ffn_norm': out['v_ffn_norm'], 'v_ffn_w_up': out['v_ffn_w_up'], 'v_ffn_conv': out['v_ffn_conv'], 'v_ffn_w_down': out['v_ffn_w_down']}


def _loss(weights, diff, rest, loss_target):
    with _jax.named_scope("forward"):
        args = {**rest, TWIN_DIFF_INPUT: diff, **{k: w.astype(_WEIGHT_DTYPES[k]) for k, w in weights.items()}}
        y = _forward(args)
    with _jax.named_scope("loss_head"):
        err = _jnp.square(y.astype(_jnp.float32) - loss_target)
        return 0.5 * _jnp.sum(_jnp.mean(err, axis=-1)) if err.ndim else 0.5 * err


def _adamw(w, g, m, v):
    m = ADAM_B1 * m + (1.0 - ADAM_B1) * g
    v = ADAM_B2 * v + (1.0 - ADAM_B2) * _jnp.square(g)
    m_hat = m / (1.0 - ADAM_B1 ** ADAM_STEP)
    v_hat = v / (1.0 - ADAM_B2 ** ADAM_STEP)
    delta = -ADAM_LR * (m_hat / (_jnp.sqrt(v_hat) + ADAM_EPS) + ADAM_WD * w)
    return delta, m, v


def reference(x, a_norm, a_w_in, a_conv, a_log, a_dt_bias, a_out_norm, a_w_out, kv_norm, w_kv, k_norm, b_norm, b_w_q, q_norm, b_w_out, ffn_norm, ffn_w_up, ffn_conv, ffn_w_down, loss_target, m_a_norm, m_a_w_in, m_a_conv, m_a_log, m_a_dt_bias, m_a_out_norm, m_a_w_out, m_kv_norm, m_w_kv, m_k_norm, m_b_norm, m_b_w_q, m_q_norm, m_b_w_out, m_ffn_norm, m_ffn_w_up, m_ffn_conv, m_ffn_w_down, v_a_norm, v_a_w_in, v_a_conv, v_a_log, v_a_dt_bias, v_a_out_norm, v_a_w_out, v_kv_norm, v_w_kv, v_k_norm, v_b_norm, v_b_w_q, v_q_norm, v_b_w_out, v_ffn_norm, v_ffn_w_up, v_ffn_conv, v_ffn_w_down):
    given = dict(x=x, a_norm=a_norm, a_w_in=a_w_in, a_conv=a_conv, a_log=a_log, a_dt_bias=a_dt_bias, a_out_norm=a_out_norm, a_w_out=a_w_out, kv_norm=kv_norm, w_kv=w_kv, k_norm=k_norm, b_norm=b_norm, b_w_q=b_w_q, q_norm=q_norm, b_w_out=b_w_out, ffn_norm=ffn_norm, ffn_w_up=ffn_w_up, ffn_conv=ffn_conv, ffn_w_down=ffn_w_down, loss_target=loss_target, m_a_norm=m_a_norm, m_a_w_in=m_a_w_in, m_a_conv=m_a_conv, m_a_log=m_a_log, m_a_dt_bias=m_a_dt_bias, m_a_out_norm=m_a_out_norm, m_a_w_out=m_a_w_out, m_kv_norm=m_kv_norm, m_w_kv=m_w_kv, m_k_norm=m_k_norm, m_b_norm=m_b_norm, m_b_w_q=m_b_w_q, m_q_norm=m_q_norm, m_b_w_out=m_b_w_out, m_ffn_norm=m_ffn_norm, m_ffn_w_up=m_ffn_w_up, m_ffn_conv=m_ffn_conv, m_ffn_w_down=m_ffn_w_down, v_a_norm=v_a_norm, v_a_w_in=v_a_w_in, v_a_conv=v_a_conv, v_a_log=v_a_log, v_a_dt_bias=v_a_dt_bias, v_a_out_norm=v_a_out_norm, v_a_w_out=v_a_w_out, v_kv_norm=v_kv_norm, v_w_kv=v_w_kv, v_k_norm=v_k_norm, v_b_norm=v_b_norm, v_b_w_q=v_b_w_q, v_q_norm=v_q_norm, v_b_w_out=v_b_w_out, v_ffn_norm=v_ffn_norm, v_ffn_w_up=v_ffn_w_up, v_ffn_conv=v_ffn_conv, v_ffn_w_down=v_ffn_w_down)
    weights = {n: given[n] for n in TWIN_WEIGHTS}
    shared = {n: given[n] for n in SHARED_INPUTS}
    per_example = {n: given[n] for n in ['x']}
    grad_fn = _jax.value_and_grad(_loss, argnums=(0, 1))

    def one_microbatch(ex, loss_target):
        ex = dict(ex)
        diff = ex.pop(TWIN_DIFF_INPUT)
        return grad_fn(weights, diff, {**shared, **ex}, loss_target)

    if N_MICROBATCH == 1:
        loss, (grad_w, grad_x) = one_microbatch(per_example, given["loss_target"])
    else:
        def body(carry, xs):
            loss_sum, grad_sum = carry
            l_k, (gw_k, gx_k) = one_microbatch(xs[0], xs[1])
            with _jax.named_scope("update"):
                return (loss_sum + l_k, _jax.tree.map(_jnp.add, grad_sum, gw_k)), gx_k

        init = (_jnp.zeros((), _jnp.float32), _jax.tree.map(_jnp.zeros_like, weights))
        (loss, grad_w), grad_x = _jax.lax.scan(body, init, (per_example, given["loss_target"]))
    with _jax.named_scope("update"):
        delta_w, new_m, new_v = {}, {}, {}
        for n in TWIN_WEIGHTS:
            delta_w[n], new_m[n], new_v[n] = _adamw(weights[n], grad_w[n], given["m_" + n], given["v_" + n])
    return (loss, grad_x, *[grad_w[n] for n in TWIN_WEIGHTS], *[delta_w[n] for n in TWIN_WEIGHTS],
            *[new_m[n] for n in TWIN_WEIGHTS], *[new_v[n] for n in TWIN_WEIGHTS])
```

```python
import functools
import math

import jax
import jax.numpy as jnp
from jax import lax
from jax.experimental import pallas as pl
from jax.experimental.pallas import tpu as pltpu

F32 = jnp.float32
BF16 = jnp.bfloat16
HIGHEST = lax.Precision.HIGHEST
MESH = pl.DeviceIdType.MESH

N_DEV = 8
LANES = 128
SUBLANES = 8
PACK_COLS = 1024
PACK_ROW_ALIGN = 16
HEAD_DIM = 128
GDN_CHUNK = 128
EPS = 1e-6
ATTN_TQ = 256
ATTN_TK = 128
EXP_UNDERFLOW = 104.0
VMEM_LIMIT = 48 * 1024 * 1024

ADAM_LR = 0.001
ADAM_B1 = 0.9
ADAM_B2 = 0.999
ADAM_EPS = 1e-08
ADAM_WD = 0.01
ADAM_STEP = 10


def _tile(n, pref):
    if n <= pref:
        return n
    best = 0
    for t in range(LANES, pref + 1, LANES):
        if n % t == 0:
            best = t
    assert best, (n, pref)
    return best


def _params(*sem):
    return pltpu.CompilerParams(dimension_semantics=tuple(sem), vmem_limit_bytes=VMEM_LIMIT)


_DIMS = {"nn": (((1,), (0,)), ((), ())), "nt": (((1,), (1,)), ((), ())), "tn": (((0,), (0,)), ((), ()))}


def _split3(x):
    hi = x.astype(BF16)
    r1 = x - hi.astype(F32)
    mid = r1.astype(BF16)
    lo = (r1 - mid.astype(F32)).astype(BF16)
    return hi, mid, lo


def _raw_dot(mode, prec, a, b):
    if prec is None:
        return lax.dot_general(a, b, _DIMS[mode], preferred_element_type=F32)
    a3, b3 = _split3(a), _split3(b)
    out = None
    for i, j in ((1, 1), (0, 2), (2, 0), (0, 1), (1, 0), (0, 0)):
        term = lax.dot_general(a3[i], b3[j], _DIMS[mode], preferred_element_type=F32)
        out = term if out is None else out + term
    return out


@functools.partial(jax.custom_vjp, nondiff_argnums=(0, 1))
def _dot(mode, prec, a, b):
    return _raw_dot(mode, prec, a, b)


def _dot_fwd(mode, prec, a, b):
    return _raw_dot(mode, prec, a, b), (a, b)


def _dot_bwd(mode, prec, res, ct):
    a, b = res
    if mode == "nn":
        return _dot("nt", prec, ct, b), _dot("tn", prec, a, ct)
    if mode == "nt":
        return _dot("nn", prec, ct, b), _dot("tn", prec, ct, a)
    return _dot("nt", prec, b, ct), _dot("nn", prec, a, ct)


_dot.defvjp(_dot_fwd, _dot_bwd)


def _softplus(x):
    return jnp.maximum(x, 0.0) + jnp.log(1.0 + jnp.exp(-jnp.abs(x)))


def _matmul(a, b, mode, out_dtype, name, residual=None):
    if mode == "nn":
        (M, K), N = a.shape, b.shape[1]
    elif mode == "nt":
        (M, K), N = a.shape, b.shape[0]
    else:
        (K, M), N = a.shape, b.shape[1]
    tm, tn, tk = _tile(M, 512), _tile(N, 512), _tile(K, 512)
    nk = K // tk

    def body(*refs):
        if residual is None:
            a_ref, b_ref, o_ref, acc_ref = refs
        else:
            a_ref, b_ref, r_ref, o_ref, acc_ref = refs
        k = pl.program_id(2)

        @pl.when(k == 0)
        def _():
            acc_ref[...] = jnp.zeros_like(acc_ref)

        acc_ref[...] += _raw_dot(mode, None, a_ref[...], b_ref[...])

        @pl.when(k == nk - 1)
        def _():
            r = acc_ref[...]
            if residual is not None:
                r = r + r_ref[...]
            o_ref[...] = r.astype(out_dtype)

    a_spec = pl.BlockSpec((tk, tm), lambda i, j, k: (k, i)) if mode == "tn" else pl.BlockSpec((tm, tk), lambda i, j, k: (i, k))
    b_spec = pl.BlockSpec((tn, tk), lambda i, j, k: (j, k)) if mode == "nt" else pl.BlockSpec((tk, tn), lambda i, j, k: (k, j))
    o_spec = pl.BlockSpec((tm, tn), lambda i, j, k: (i, j))
    in_specs, args = [a_spec, b_spec], [a, b]
    if residual is not None:
        in_specs.append(o_spec)
        args.append(residual)
    return pl.pallas_call(
        body, name=name, grid=(M // tm, N // tn, nk), in_specs=in_specs, out_specs=o_spec,
        out_shape=jax.ShapeDtypeStruct((M, N), out_dtype), scratch_shapes=[pltpu.VMEM((tm, tn), F32)],
        compiler_params=_params("parallel", "parallel", "arbitrary"),
    )(*args)


def _rmsnorm_fwd(x, g, name):
    T, D = x.shape
    tt = _tile(T, 512)

    def body(x_ref, g_ref, o_ref):
        xv = x_ref[...]
        r = lax.rsqrt(jnp.mean(xv * xv, axis=-1, keepdims=True) + EPS)
        o_ref[...] = (xv * r * g_ref[...]).astype(BF16)

    row = pl.BlockSpec((tt, D), lambda i: (i, 0))
    return pl.pallas_call(
        body, name=name, grid=(T // tt,), in_specs=[row, pl.BlockSpec((1, D), lambda i: (0, 0))], out_specs=row,
        out_shape=jax.ShapeDtypeStruct((T, D), BF16), compiler_params=_params("parallel"),
    )(x, g)


def _rmsnorm_bwd(x, g, dh, dx_in, name):
    T, D = x.shape
    tt = _tile(T, 512)

    def body(x_ref, g_ref, dh_ref, dxin_ref, dx_ref, dxb_ref, dg_ref):
        xv = x_ref[...]
        dhv = dh_ref[...].astype(F32)
        r = lax.rsqrt(jnp.mean(xv * xv, axis=-1, keepdims=True) + EPS)
        dn = dhv * g_ref[...]
        dx = r * dn - xv * (r * r * r) * jnp.mean(dn * xv, axis=-1, keepdims=True) + dxin_ref[...]
        dx_ref[...] = dx
        dxb_ref[...] = dx.astype(BF16)

        @pl.when(pl.program_id(0) == 0)
        def _():
            dg_ref[...] = jnp.zeros_like(dg_ref)

        dg_ref[...] += jnp.sum(dhv * xv * r, axis=0, keepdims=True)

    row = pl.BlockSpec((tt, D), lambda i: (i, 0))
    vec = pl.BlockSpec((1, D), lambda i: (0, 0))
    return pl.pallas_call(
        body, name=name, grid=(T // tt,), in_specs=[row, vec, row, row], out_specs=[row, row, vec],
        out_shape=[jax.ShapeDtypeStruct((T, D), F32), jax.ShapeDtypeStruct((T, D), BF16), jax.ShapeDtypeStruct((1, D), F32)],
        compiler_params=_params("arbitrary"),
    )(x, g, dh, dx_in)


def _headnorm(x, gain):
    outs = []
    for h in range(x.shape[1] // HEAD_DIM):
        xh = x[:, h * HEAD_DIM:(h + 1) * HEAD_DIM]
        outs.append(xh * lax.rsqrt(jnp.mean(xh * xh, axis=-1, keepdims=True) + EPS) * gain)
    return jnp.concatenate(outs, axis=1)


def _qnorm_fwd(qp, gain, name):
    T, D = qp.shape
    tt = _tile(T, 512)

    def body(x_ref, g_ref, o_ref):
        o_ref[...] = _headnorm(x_ref[...], g_ref[...]).astype(BF16)

    row = pl.BlockSpec((tt, D), lambda i: (i, 0))
    return pl.pallas_call(
        body, name=name, grid=(T // tt,), in_specs=[row, pl.BlockSpec((1, HEAD_DIM), lambda i: (0, 0))], out_specs=row,
        out_shape=jax.ShapeDtypeStruct((T, D), BF16), compiler_params=_params("parallel"),
    )(qp, gain)


def _qnorm_bwd(qp, gain, dq, name):
    T, D = qp.shape
    tt = _tile(T, 512)

    def body(x_ref, g_ref, dq_ref, dx_ref, dg_ref):
        _, vjp = jax.vjp(_headnorm, x_ref[...], g_ref[...])
        dx, dg = vjp(dq_ref[...])
        dx_ref[...] = dx.astype(BF16)

        @pl.when(pl.program_id(0) == 0)
        def _():
            dg_ref[...] = jnp.zeros_like(dg_ref)

        dg_ref[...] += dg

    row = pl.BlockSpec((tt, D), lambda i: (i, 0))
    vec = pl.BlockSpec((1, HEAD_DIM), lambda i: (0, 0))
    return pl.pallas_call(
        body, name=name, grid=(T // tt,), in_specs=[row, vec, row], out_specs=[row, vec],
        out_shape=[jax.ShapeDtypeStruct((T, D), BF16), jax.ShapeDtypeStruct((1, HEAD_DIM), F32)],
        compiler_params=_params("arbitrary"),
    )(qp, gain, dq)


def _kv_post_fwd(kv, gain, name):
    T, D2 = kv.shape
    D = D2 // 2
    tt = _tile(T, 512)

    def body(k_ref, v_ref, g_ref, ko_ref, vo_ref):
        ko_ref[...] = _headnorm(k_ref[...], g_ref[...]).astype(BF16)
        vo_ref[...] = v_ref[...].astype(BF16)

    row = pl.BlockSpec((tt, D), lambda i: (i, 0))
    return pl.pallas_call(
        body, name=name, grid=(T // tt,),
        in_specs=[row, pl.BlockSpec((tt, D), lambda i: (i, 1)), pl.BlockSpec((1, HEAD_DIM), lambda i: (0, 0))],
        out_specs=[row, row], out_shape=[jax.ShapeDtypeStruct((T, D), BF16)] * 2, compiler_params=_params("parallel"),
    )(kv, kv, gain)


def _kv_post_bwd(kv, gain, dk_a, dk_b, dv_a, dv_b, name):
    T, D2 = kv.shape
    D = D2 // 2
    tt = _tile(T, 512)

    def body(k_ref, g_ref, dka_ref, dkb_ref, dva_ref, dvb_ref, dkv_ref, dg_ref):
        _, vjp = jax.vjp(_headnorm, k_ref[...], g_ref[...])
        dx, dg = vjp(dka_ref[...] + dkb_ref[...])
        dkv_ref[:, :D] = dx.astype(BF16)
        dkv_ref[:, D:] = (dva_ref[...] + dvb_ref[...]).astype(BF16)

        @pl.when(pl.program_id(0) == 0)
        def _():
            dg_ref[...] = jnp.zeros_like(dg_ref)

        dg_ref[...] += dg

    row = pl.BlockSpec((tt, D), lambda i: (i, 0))
    vec = pl.BlockSpec((1, HEAD_DIM), lambda i: (0, 0))
    return pl.pallas_call(
        body, name=name, grid=(T // tt,), in_specs=[row, vec, row, row, row, row],
        out_specs=[pl.BlockSpec((tt, D2), lambda i: (i, 0)), vec],
        out_shape=[jax.ShapeDtypeStruct((T, D2), BF16), jax.ShapeDtypeStruct((1, HEAD_DIM), F32)],
        compiler_params=_params("arbitrary"),
    )(kv, gain, dk_a, dk_b, dv_a, dv_b)


def _loss_grad(y, target, name):
    T, D = y.shape
    tt = _tile(T, 512)
    nt = T // tt

    def body(y_ref, t_ref, loss_ref, dy_ref, dyb_ref, acc_ref):
        i = pl.program_id(0)
        d = y_ref[...] - t_ref[...]
        dy = d * (1.0 / D)
        dy_ref[...] = dy
        dyb_ref[...] = dy.astype(BF16)

        @pl.when(i == 0)
        def _():
            acc_ref[...] = jnp.zeros_like(acc_ref)

        acc_ref[...] += jnp.sum(d * d, axis=0, keepdims=True)

        @pl.when(i == nt - 1)
        def _():
            loss_ref[...] = jnp.zeros_like(loss_ref) + jnp.sum(acc_ref[...]) * (0.5 / D)

    row = pl.BlockSpec((tt, D), lambda i: (i, 0))
    return pl.pallas_call(
        body, name=name, grid=(nt,), in_specs=[row, row], out_specs=[pl.BlockSpec((1, LANES), lambda i: (0, 0)), row, row],
        out_shape=[jax.ShapeDtypeStruct((1, LANES), F32), jax.ShapeDtypeStruct((T, D), F32), jax.ShapeDtypeStruct((T, D), BF16)],
        scratch_shapes=[pltpu.VMEM((1, D), F32)], compiler_params=_params("arbitrary"),
    )(y, target)


def _conv_tile(halo, main, w):
    K = w.shape[0]
    xc = jnp.concatenate([halo, main], axis=0)
    y = main * w[K - 1:K, :]
    for k in range(K - 1):
        y = y + pltpu.roll(xc, K - 1 - k, 0)[SUBLANES:, :] * w[k:k + 1, :]
    return y


def _prev_halo(tt, col):
    return lambda cb: pl.BlockSpec((SUBLANES, cb), lambda j, i: (jnp.maximum(i * (tt // SUBLANES) - 1, 0), col(j)))


def _conv_bwd(dy, x, w, name):
    T, C = dy.shape
    K = w.shape[0]
    tt, cb = _tile(T, 256), _tile(C, 1024)
    nt = T // tt

    def body(dy_ref, dyn_ref, x_ref, xp_ref, w_ref, dx_ref, dw_ref):
        i = pl.program_id(1)
        wv = w_ref[...]
        dyv = dy_ref[...]
        nxt = jnp.where(i == nt - 1, 0.0, dyn_ref[...])
        prev = jnp.where(i == 0, 0.0, xp_ref[...])
        dyc = jnp.concatenate([dyv, nxt], axis=0)
        xc = jnp.concatenate([prev, x_ref[...]], axis=0)
        tap = lax.broadcasted_iota(jnp.int32, (K, cb), 0)
        dx = dyv * wv[K - 1:K, :]
        dw = jnp.where(tap == K - 1, jnp.sum(dyv * x_ref[...], axis=0, keepdims=True), 0.0)
        for k in range(K - 1):
            s = K - 1 - k
            dx = dx + pltpu.roll(dyc, SUBLANES - s, 0)[SUBLANES:, :] * wv[k:k + 1, :]
            dw = dw + jnp.where(tap == k, jnp.sum(dyv * pltpu.roll(xc, s, 0)[SUBLANES:, :], axis=0, keepdims=True), 0.0)
        dx_ref[...] = dx.astype(BF16)

        @pl.when(i == 0)
        def _():
            dw_ref[...] = jnp.zeros_like(dw_ref)

        dw_ref[...] += dw

    main = pl.BlockSpec((tt, cb), lambda j, i: (i, j))
    nxt = pl.BlockSpec((SUBLANES, cb), lambda j, i: (jnp.minimum((i + 1) * (tt // SUBLANES), T // SUBLANES - 1), j))
    wspec = pl.BlockSpec((K, cb), lambda j, i: (0, j))
    return pl.pallas_call(
        body, name=name, grid=(C // cb, nt), in_specs=[main, nxt, main, _prev_halo(tt, lambda j: j)(cb), wspec],
        out_specs=[main, wspec], out_shape=[jax.ShapeDtypeStruct((T, C), BF16), jax.ShapeDtypeStruct((K, C), F32)],
        compiler_params=_params("parallel", "arbitrary"),
    )(dy, dy, x, x, w)


def _ffn_mid_specs(T, F):
    tt, cb = _tile(T, 256), _tile(F, 512)
    nf = F // cb
    gate = pl.BlockSpec((tt, cb), lambda j, i: (i, j % nf))
    up = pl.BlockSpec((tt, cb), lambda j, i: (i, nf + j % nf))
    gate_h = _prev_halo(tt, lambda j: j % nf)(cb)
    up_h = _prev_halo(tt, lambda j: nf + j % nf)(cb)
    wg = pl.BlockSpec((3, cb), lambda j, i: (0, j % nf))
    wu = pl.BlockSpec((3, cb), lambda j, i: (0, nf + j % nf))
    return tt, cb, nf, [gate, gate_h, up, up_h, wg, wu]


def _ffn_mid_fwd(u_pre, w, name):
    T, F2 = u_pre.shape
    F = F2 // 2
    tt, cb, nf, specs = _ffn_mid_specs(T, F)

    def body(g_ref, gh_ref, u_ref, uh_ref, wg_ref, wu_ref, o_ref):
        first = pl.program_id(1) == 0
        ug = _conv_tile(jnp.where(first, 0.0, gh_ref[...]), g_ref[...], wg_ref[...])
        uu = _conv_tile(jnp.where(first, 0.0, uh_ref[...]), u_ref[...], wu_ref[...])
        o_ref[...] = (ug * jax.nn.sigmoid(ug) * uu).astype(BF16)

    return pl.pallas_call(
        body, name=name, grid=(nf, T // tt), in_specs=specs, out_specs=pl.BlockSpec((tt, cb), lambda j, i: (i, j)),
        out_shape=jax.ShapeDtypeStruct((T, F), BF16), compiler_params=_params("parallel", "parallel"),
    )(u_pre, u_pre, u_pre, u_pre, w, w)


def _ffn_mid_bwd(u_pre, w, dact, name):
    T, F2 = u_pre.shape
    F = F2 // 2
    tt, cb, nf, specs = _ffn_mid_specs(T, F)

    def body(g_ref, gh_ref, u_ref, uh_ref, wg_ref, wu_ref, da_ref, o_ref):
        first = pl.program_id(1) == 0
        ug = _conv_tile(jnp.where(first, 0.0, gh_ref[...]), g_ref[...], wg_ref[...])
        uu = _conv_tile(jnp.where(first, 0.0, uh_ref[...]), u_ref[...], wu_ref[...])
        sg = jax.nn.sigmoid(ug)
        da = da_ref[...]
        d_gate = da * uu * sg * (1.0 + ug * (1.0 - sg))
        d_up = da * ug * sg
        o_ref[...] = jnp.where(pl.program_id(0) < nf, d_gate, d_up)

    return pl.pallas_call(
        body, name=name, grid=(2 * nf, T // tt), in_specs=specs + [pl.BlockSpec((tt, cb), lambda j, i: (i, j % nf))],
        out_specs=pl.BlockSpec((tt, cb), lambda j, i: (i, j)), out_shape=jax.ShapeDtypeStruct((T, F2), F32),
        compiler_params=_params("parallel", "parallel"),
    )(u_pre, u_pre, u_pre, u_pre, w, w, dact)


def _gdn_act(c, ab, alog, dtb, H):
    D = H * HEAD_DIM
    s = c * jax.nn.sigmoid(c)
    qs, ks = [], []
    for h in range(H):
        qh = s[:, h * HEAD_DIM:(h + 1) * HEAD_DIM]
        kh = s[:, D + h * HEAD_DIM:D + (h + 1) * HEAD_DIM]
        qs.append(qh * lax.rsqrt(jnp.sum(qh * qh, axis=-1, keepdims=True) + EPS) * (HEAD_DIM ** -0.5))
        ks.append(kh * lax.rsqrt(jnp.sum(kh * kh, axis=-1, keepdims=True) + EPS))
    g = -jnp.exp(alog) * _softplus(ab + dtb)
    lane = lax.broadcasted_iota(jnp.int32, ab.shape, 1)
    gates = jnp.where(lane < H, g, jax.nn.sigmoid(ab))
    return jnp.concatenate(qs, axis=1), jnp.concatenate(ks, axis=1), s[:, 2 * D:], gates


def _gdn_pre_specs(T, D):
    tt = _tile(T, 256)
    main = pl.BlockSpec((tt, 3 * D), lambda i: (i, 0))
    halo = pl.BlockSpec((SUBLANES, 3 * D), lambda i: (jnp.maximum(i * (tt // SUBLANES) - 1, 0), 0))
    ab = pl.BlockSpec((tt, LANES), lambda i: (i, 4 * D // LANES))
    return tt, main, halo, ab


def _gdn_pre_fwd(proj, conv_w, alog, dtb, H, name):
    T = proj.shape[0]
    D = H * HEAD_DIM
    tt, main, halo, ab = _gdn_pre_specs(T, D)
    K = conv_w.shape[0]

    def body(x_ref, xh_ref, ab_ref, w_ref, al_ref, dt_ref, q_ref, k_ref, v_ref, g_ref):
        c = _conv_tile(jnp.where(pl.program_id(0) == 0, 0.0, xh_ref[...]), x_ref[...], w_ref[...])
        q, k, v, gates = _gdn_act(c, ab_ref[...], al_ref[...], dt_ref[...], H)
        q_ref[...] = q
        k_ref[...] = k
        v_ref[...] = v
        g_ref[...] = gates

    row = pl.BlockSpec((tt, D), lambda i: (i, 0))
    vec = pl.BlockSpec((1, LANES), lambda i: (0, 0))
    gspec = pl.BlockSpec((tt, LANES), lambda i: (i, 0))
    return pl.pallas_call(
        body, name=name, grid=(T // tt,),
        in_specs=[main, halo, ab, pl.BlockSpec((K, 3 * D), lambda i: (0, 0)), vec, vec], out_specs=[row, row, row, gspec],
        out_shape=[jax.ShapeDtypeStruct((T, D), F32)] * 3 + [jax.ShapeDtypeStruct((T, LANES), F32)],
        compiler_params=_params("parallel"),
    )(proj, proj, proj, conv_w, alog, dtb)


def _gdn_pre_bwd(proj, conv_w, alog, dtb, dq, dk, dv, dgates, H, name):
    T = proj.shape[0]
    D = H * HEAD_DIM
    tt, main, halo, ab = _gdn_pre_specs(T, D)
    K = conv_w.shape[0]

    def body(x_ref, xh_ref, ab_ref, w_ref, al_ref, dt_ref, dq_ref, dk_ref, dv_ref, dg_ref, dc_ref, dab_ref, dal_ref, ddt_ref):
        c = _conv_tile(jnp.where(pl.program_id(0) == 0, 0.0, xh_ref[...]), x_ref[...], w_ref[...])
        _, vjp = jax.vjp(functools.partial(_gdn_act, H=H), c, ab_ref[...], al_ref[...], dt_ref[...])
        dc, dab, dal, ddt = vjp((dq_ref[...], dk_ref[...], dv_ref[...], dg_ref[...]))
        dc_ref[...] = dc
        dab_ref[...] = dab.astype(BF16)

        @pl.when(pl.program_id(0) == 0)
        def _():
            dal_ref[...] = jnp.zeros_like(dal_ref)
            ddt_ref[...] = jnp.zeros_like(ddt_ref)

        dal_ref[...] += dal
        ddt_ref[...] += ddt

    row = pl.BlockSpec((tt, D), lambda i: (i, 0))
    vec = pl.BlockSpec((1, LANES), lambda i: (0, 0))
    gspec = pl.BlockSpec((tt, LANES), lambda i: (i, 0))
    return pl.pallas_call(
        body, name=name, grid=(T // tt,),
        in_specs=[main, halo, ab, pl.BlockSpec((K, 3 * D), lambda i: (0, 0)), vec, vec, row, row, row, gspec],
        out_specs=[main, gspec, vec, vec],
        out_shape=[jax.ShapeDtypeStruct((T, 3 * D), F32), jax.ShapeDtypeStruct((T, LANES), BF16),
                   jax.ShapeDtypeStruct((1, LANES), F32), jax.ShapeDtypeStruct((1, LANES), F32)],
        compiler_params=_params("arbitrary"),
    )(proj, proj, proj, conv_w, alog, dtb, dq, dk, dv, dgates)


def _chunk_masks():
    C = GDN_CHUNK
    i = lax.broadcasted_iota(jnp.int32, (C, C), 0)
    j = lax.broadcasted_iota(jnp.int32, (C, C), 1)
    levels = []
    for lv in range(int(math.log2(C))):
        levels.append((lax.shift_right_logical(i, lv + 1) == lax.shift_right_logical(j, lv + 1))
                      & ((lax.shift_right_logical(i, lv) & 1) == 1) & ((lax.shift_right_logical(j, lv) & 1) == 0))
    return dict(tri=i >= j, strict=i > j, eye=i == j, levels=levels,
                lane=lax.broadcasted_iota(jnp.int32, (C, LANES), 1), last_row=lax.broadcasted_iota(jnp.int32, (C, 1), 0) == C - 1)


@jax.custom_vjp
def _unit_lower_inverse(L):
    m = _chunk_masks()
    A = m["eye"].astype(F32)
    for lvl in m["levels"]:
        A = A - _raw_dot("nn", HIGHEST, _raw_dot("nn", HIGHEST, A, jnp.where(lvl, L, 0.0)), A)
    return A


def _unit_lower_inverse_fwd(L):
    A = _unit_lower_inverse(L)
    return A, A


def _unit_lower_inverse_bwd(A, ct):
    return (-_raw_dot("nt", HIGHEST, _raw_dot("tn", HIGHEST, A, ct), A),)


_unit_lower_inverse.defvjp(_unit_lower_inverse_fwd, _unit_lower_inverse_bwd)


def _gdn_chunks(q, k, v, gates, S, H, m):
    C = GDN_CHUNK
    tri_f = m["tri"].astype(F32)
    eye_f = m["eye"].astype(F32)
    gam_all = _dot("nn", HIGHEST, tri_f, gates)
    outs, states = [], []
    for h in range(H):
        sl = slice(h * HEAD_DIM, (h + 1) * HEAD_DIM)
        qh, kh, vh, Sh = q[:, sl], k[:, sl], v[:, sl], S[:, sl]
        gam = jnp.sum(jnp.where(m["lane"] == h, gam_all, 0.0), axis=1, keepdims=True)
        beta = jnp.sum(jnp.where(m["lane"] == H + h, gates, 0.0), axis=1, keepdims=True)
        gam_row = _dot("nn", HIGHEST, jnp.ones((C, C), F32), jnp.where(m["eye"], gam, 0.0))
        decay = jnp.where(m["tri"], jnp.exp(jnp.where(m["tri"], gam - gam_row, 0.0)), 0.0)
        kb = kh * beta
        L = jnp.where(m["strict"], _dot("nt", None, kb, kh) * decay, 0.0)
        A = _unit_lower_inverse(L)
        eg = jnp.exp(gam)
        u = _dot("nn", HIGHEST, A, vh * beta)
        w = _dot("nn", HIGHEST, A, kb * eg)
        qk = jnp.where(m["tri"], _dot("nt", None, qh, kh) * decay, 0.0)
        g_last = jnp.sum(jnp.where(m["last_row"], gam, 0.0), axis=0, keepdims=True)
        v_new = u - _dot("nn", None, w, Sh)
        outs.append(_dot("nn", None, qh * eg, Sh) + _dot("nn", None, qk, v_new))
        states.append(Sh * jnp.exp(g_last) + _dot("tn", None, kh * jnp.exp(g_last - gam), v_new))
    return jnp.concatenate(outs, axis=1), jnp.concatenate(states, axis=1)


def _gdn_rule_fwd(q, k, v, gates, H, name):
    T, D = q.shape
    C = GDN_CHUNK

    def body(q_ref, k_ref, v_ref, g_ref, o_ref, sh_ref, s_ref):
        @pl.when(pl.program_id(0) == 0)
        def _():
            s_ref[...] = jnp.zeros_like(s_ref)

        S = s_ref[...]
        sh_ref[...] = S
        o, S_new = _gdn_chunks(q_ref[...], k_ref[...], v_ref[...], g_ref[...], S, H, _chunk_masks())
        o_ref[...] = o
        s_ref[...] = S_new

    row = pl.BlockSpec((C, D), lambda n: (n, 0))
    return pl.pallas_call(
        body, name=name, grid=(T // C,), in_specs=[row, row, row, pl.BlockSpec((C, LANES), lambda n: (n, 0))],
        out_specs=[row, pl.BlockSpec((None, HEAD_DIM, D), lambda n: (n, 0, 0))],
        out_shape=[jax.ShapeDtypeStruct((T, D), F32), jax.ShapeDtypeStruct((T // C, HEAD_DIM, D), F32)],
        scratch_shapes=[pltpu.VMEM((HEAD_DIM, D), F32)], compiler_params=_params("arbitrary"),
    )(q, k, v, gates)


def _gdn_rule_bwd(q, k, v, gates, s_hist, do, H, name):
    T, D = q.shape
    C = GDN_CHUNK
    N = T // C

    def body(q_ref, k_ref, v_ref, g_ref, s_ref, do_ref, dq_ref, dk_ref, dv_ref, dg_ref, ds_ref):
        @pl.when(pl.program_id(0) == 0)
        def _():
            ds_ref[...] = jnp.zeros_like(ds_ref)

        fn = functools.partial(_gdn_chunks, H=H, m=_chunk_masks())
        _, vjp = jax.vjp(fn, q_ref[...], k_ref[...], v_ref[...], g_ref[...], s_ref[...])
        dq, dk, dv, dg, ds = vjp((do_ref[...], ds_ref[...]))
        dq_ref[...] = dq
        dk_ref[...] = dk
        dv_ref[...] = dv
        dg_ref[...] = dg
        ds_ref[...] = ds

    row = pl.BlockSpec((C, D), lambda n: (N - 1 - n, 0))
    gspec = pl.BlockSpec((C, LANES), lambda n: (N - 1 - n, 0))
    return pl.pallas_call(
        body, name=name, grid=(N,),
        in_specs=[row, row, row, gspec, pl.BlockSpec((None, HEAD_DIM, D), lambda n: (N - 1 - n, 0, 0)), row],
        out_specs=[row, row, row, gspec],
        out_shape=[jax.ShapeDtypeStruct((T, D), F32)] * 3 + [jax.ShapeDtypeStruct((T, LANES), F32)],
        scratch_shapes=[pltpu.VMEM((HEAD_DIM, D), F32)], compiler_params=_params("arbitrary"),
    )(q, k, v, gates, s_hist, do)


def _gdn_gate(o, z, gain):
    return _headnorm(o, gain) * (z * jax.nn.sigmoid(z))


def _gdn_post_fwd(o, proj, gain, name):
    T, D = o.shape
    tt = _tile(T, 512)

    def body(o_ref, z_ref, g_ref, y_ref):
        y_ref[...] = _gdn_gate(o_ref[...], z_ref[...], g_ref[...]).astype(BF16)

    row = pl.BlockSpec((tt, D), lambda i: (i, 0))
    return pl.pallas_call(
        body, name=name, grid=(T // tt,),
        in_specs=[row, pl.BlockSpec((tt, D), lambda i: (i, 3)), pl.BlockSpec((1, HEAD_DIM), lambda i: (0, 0))], out_specs=row,
        out_shape=jax.ShapeDtypeStruct((T, D), BF16), compiler_params=_params("parallel"),
    )(o, proj, gain)


def _gdn_post_bwd(o, proj, gain, dy, name):
    T, D = o.shape
    tt = _tile(T, 512)

    def body(o_ref, z_ref, g_ref, dy_ref, do_ref, dz_ref, dg_ref):
        _, vjp = jax.vjp(_gdn_gate, o_ref[...], z_ref[...], g_ref[...])
        do, dz, dg = vjp(dy_ref[...])
        do_ref[...] = do
        dz_ref[...] = dz.astype(BF16)

        @pl.when(pl.program_id(0) == 0)
        def _():
            dg_ref[...] = jnp.zeros_like(dg_ref)

        dg_ref[...] += dg

    row = pl.BlockSpec((tt, D), lambda i: (i, 0))
    vec = pl.BlockSpec((1, HEAD_DIM), lambda i: (0, 0))
    return pl.pallas_call(
        body, name=name, grid=(T // tt,), in_specs=[row, pl.BlockSpec((tt, D), lambda i: (i, 3)), vec, row],
        out_specs=[row, row, vec],
        out_shape=[jax.ShapeDtypeStruct((T, D), F32), jax.ShapeDtypeStruct((T, D), BF16), jax.ShapeDtypeStruct((1, HEAD_DIM), F32)],
        compiler_params=_params("arbitrary"),
    )(o, proj, gain, dy)


def _scan_matmul(x, mat):
    hi, mid, lo = _split3(x)
    return (_raw_dot("nn", None, hi, mat) + _raw_dot("nn", None, mid, mat)) + _raw_dot("nn", None, lo, mat)


def _attn_tile(q, k_ref, j, row0):
    ks = k_ref[pl.ds(pl.multiple_of(j * ATTN_TK, ATTN_TK), ATTN_TK), :]
    z = _raw_dot("nt", None, q, ks) * (HEAD_DIM ** -0.5)
    row = row0 + lax.broadcasted_iota(jnp.int32, z.shape, 0)
    col = j * ATTN_TK + lax.broadcasted_iota(jnp.int32, z.shape, 1)
    causal = col < row
    l0 = -_softplus(z)
    return ks, z, causal, l0, jnp.where(causal, l0, 0.0)


def _attn_specs(T, D):
    tq = min(ATTN_TQ, T)
    qspec = pl.BlockSpec((tq, HEAD_DIM), lambda h, i: (i, h))
    kspec = pl.BlockSpec((T, HEAD_DIM), lambda h, i: (0, h))
    return tq, qspec, kspec


def _sb_attn_fwd(q, k, v, name):
    T, D = q.shape
    tq, qspec, kspec = _attn_specs(T, D)
    per = tq // ATTN_TK

    def body(q_ref, k_ref, v_ref, o_ref):
        i = pl.program_id(1)
        qv = q_ref[...]
        jj = lax.broadcasted_iota(jnp.int32, (ATTN_TK, ATTN_TK), 0)
        ss = lax.broadcasted_iota(jnp.int32, (ATTN_TK, ATTN_TK), 1)
        later = (jj > ss).astype(BF16)

        def cond(c):
            return c[3]

        def step(c):
            j, r, acc, _ = c
            _, z, causal, l0, lf = _attn_tile(qv, k_ref, j, i * tq)
            after = _scan_matmul(lf, later)
            a = jnp.where(causal, jnp.exp(z + l0 + after + r), 0.0)
            vs = v_ref[pl.ds(pl.multiple_of(j * ATTN_TK, ATTN_TK), ATTN_TK), :]
            acc = acc + _raw_dot("nn", None, a.astype(BF16), vs)
            r = r + jnp.sum(lf, axis=1, keepdims=True)
            return j - 1, r, acc, (j > 0) & (jnp.max(r) >= -EXP_UNDERFLOW)

        init = ((i + 1) * per - 1, jnp.zeros((tq, 1), F32), jnp.zeros((tq, HEAD_DIM), F32), jnp.asarray(True))
        _, _, acc, _ = lax.while_loop(cond, step, init)
        o_ref[...] = acc.astype(BF16)

    return pl.pallas_call(
        body, name=name, grid=(D // HEAD_DIM, T // tq), in_specs=[qspec, kspec, kspec], out_specs=qspec,
        out_shape=jax.ShapeDtypeStruct((T, D), BF16), compiler_params=_params("parallel", "parallel"),
    )(q, k, v)


def _sb_attn_bwd(q, k, v, do, name):
    T, D = q.shape
    tq, qspec, kspec = _attn_specs(T, D)
    per = tq // ATTN_TK
    scale = HEAD_DIM ** -0.5

    def body(q_ref, do_ref, k_ref, v_ref, dq_ref, dk_ref, dv_ref):
        i = pl.program_id(1)

        @pl.when(i == 0)
        def _():
            dk_ref[...] = jnp.zeros_like(dk_ref)
            dv_ref[...] = jnp.zeros_like(dv_ref)

        qv = q_ref[...]
        dov = do_ref[...]
        jj = lax.broadcasted_iota(jnp.int32, (ATTN_TK, ATTN_TK), 0)
        ss = lax.broadcasted_iota(jnp.int32, (ATTN_TK, ATTN_TK), 1)
        later = (jj > ss).astype(BF16)
        earlier = (jj < ss).astype(BF16)
        nkb = (i + 1) * per

        def cond(c):
            return c[2]

        def find(c):
            j, r, _ = c
            lf = _attn_tile(qv, k_ref, j, i * tq)[4]
            r = r + jnp.sum(lf, axis=1, keepdims=True)
            return j - 1, r, (j > 0) & (jnp.max(r) >= -EXP_UNDERFLOW)

        j_end, total, _ = lax.while_loop(cond, find, (nkb - 1, jnp.zeros((tq, 1), F32), jnp.asarray(True)))

        def step(j, c):
            passed, p, dq = c
            ks, z, causal, l0, lf = _attn_tile(qv, k_ref, j, i * tq)
            rows = pl.ds(pl.multiple_of(j * ATTN_TK, ATTN_TK), ATTN_TK)
            block = jnp.sum(lf, axis=1, keepdims=True)
            after = _scan_matmul(lf, later) + (total - passed - block)
            a = jnp.where(causal, jnp.exp(z + l0 + after), 0.0)
            e = a * _raw_dot("nt", None, dov, v_ref[rows, :])
            before = _scan_matmul(e, earlier) + p
            sig = jnp.exp(z + l0)
            dz = (jnp.where(causal, e * (1.0 - sig) - sig * before, 0.0) * scale).astype(BF16)
            dq = dq + _raw_dot("nn", None, dz, ks)
            dk_ref[rows, :] += _raw_dot("tn", None, dz, qv)
            dv_ref[rows, :] += _raw_dot("tn", None, a.astype(BF16), dov)
            return passed + block, p + jnp.sum(e, axis=1, keepdims=True), dq

        zero = jnp.zeros((tq, 1), F32)
        dq_ref[...] = lax.fori_loop(j_end + 1, nkb, step, (zero, zero, jnp.zeros((tq, HEAD_DIM), F32)))[2]

    return pl.pallas_call(
        body, name=name, grid=(D // HEAD_DIM, T // tq), in_specs=[qspec, qspec, kspec, kspec],
        out_specs=[qspec, kspec, kspec], out_shape=[jax.ShapeDtypeStruct((T, D), F32)] * 3,
        compiler_params=_params("parallel", "arbitrary"),
    )(q, do, k, v)


def _exchange(send, gather, name):
    shape = send.shape if gather else send.shape[1:]

    def body(x_ref, out_ref, send_sems, recv_sems, local_sem):
        x, y, c = lax.axis_index("x"), lax.axis_index("y"), lax.axis_index("c")
        me = 4 * x + 2 * y + c

        def src(slot):
            return x_ref if gather else x_ref.at[slot]

        mine = pltpu.make_async_copy(src(me), out_ref.at[me], local_sem)
        mine.start()
        copies = []
        for kk in range(1, N_DEV):
            px = 1 - x if kk & 4 else x
            py = 1 - y if kk & 2 else y
            pc = 1 - c if kk & 1 else c
            peer = 4 * px + 2 * py + pc
            out_copy = pltpu.make_async_remote_copy(
                src_ref=src(peer), dst_ref=out_ref.at[me], send_sem=send_sems.at[kk - 1], recv_sem=recv_sems.at[kk - 1],
                device_id=(px, py, pc), device_id_type=MESH)
            in_copy = pltpu.make_async_remote_copy(
                src_ref=src(me), dst_ref=out_ref.at[peer], send_sem=send_sems.at[kk - 1], recv_sem=recv_sems.at[kk - 1],
                device_id=(px, py, pc), device_id_type=MESH)
            out_copy.start()
            copies.append((out_copy, in_copy))
        for out_copy, in_copy in copies:
            out_copy.wait_send()
            in_copy.wait_recv()
        mine.wait()

    any_spec = pl.BlockSpec(memory_space=pl.ANY)
    return pl.pallas_call(
        body, name=name, in_specs=[any_spec], out_specs=any_spec,
        out_shape=jax.ShapeDtypeStruct((N_DEV,) + tuple(shape), send.dtype),
        scratch_shapes=[pltpu.SemaphoreType.DMA((N_DEV - 1,)), pltpu.SemaphoreType.DMA((N_DEV - 1,)), pltpu.SemaphoreType.DMA],
    )(send)


def _adamw(parts, w, m, v, name):
    R, C = w.shape
    tr = _tile(R, 256)

    def body(p_ref, w_ref, m_ref, v_ref, g_ref, d_ref, nm_ref, nv_ref):
        g = p_ref[0]
        for j in range(1, N_DEV):
            g = g + p_ref[j]
        nm = ADAM_B1 * m_ref[...] + (1.0 - ADAM_B1) * g
        nv = ADAM_B2 * v_ref[...] + (1.0 - ADAM_B2) * (g * g)
        m_hat = nm / (1.0 - ADAM_B1 ** ADAM_STEP)
        v_hat = nv / (1.0 - ADAM_B2 ** ADAM_STEP)
        g_ref[...] = g
        d_ref[...] = -ADAM_LR * (m_hat / (jnp.sqrt(v_hat) + ADAM_EPS) + ADAM_WD * w_ref[...])
        nm_ref[...] = nm
        nv_ref[...] = nv

    row = pl.BlockSpec((tr, C), lambda i: (i, 0))
    return pl.pallas_call(
        body, name=name, grid=(R // tr,), in_specs=[pl.BlockSpec((N_DEV, tr, C), lambda i: (0, i, 0)), row, row, row],
        out_specs=[row] * 4, out_shape=[jax.ShapeDtypeStruct((R, C), F32)] * 4, compiler_params=_params("parallel"),
    )(parts, w, m, v)


def _pack_rows(n, align):
    rows = -(-n // PACK_COLS)
    return -(-rows // align) * align


def _pack(arrays, dtype, align, total_align=None, lead=0):
    pieces = []
    for a in arrays:
        head = a.shape[:lead]
        n = math.prod(a.shape[lead:])
        rows = _pack_rows(n, align)
        flat = jnp.pad(a.astype(dtype).reshape(head + (n,)), [(0, 0)] * lead + [(0, rows * PACK_COLS - n)])
        pieces.append(flat.reshape(head + (rows, PACK_COLS)))
    out = jnp.concatenate(pieces, axis=lead)
    if total_align:
        out = jnp.pad(out, [(0, 0)] * lead + [(0, -out.shape[lead] % total_align), (0, 0)])
    return out


def _unpack(buf, shapes, align):
    outs, r0 = [], 0
    lead = buf.shape[:-2]
    for shp in shapes:
        n = math.prod(shp)
        rows = _pack_rows(n, align)
        flat = buf[..., r0:r0 + rows, :].reshape(lead + (rows * PACK_COLS,))[..., :n]
        outs.append(flat.reshape(lead + tuple(shp)))
        r0 += rows
    return outs


def _join(blocks, axis):
    g = jnp.moveaxis(blocks, 0, axis)
    shp = list(g.shape)
    return g.reshape(shp[:axis] + [shp[axis] * shp[axis + 1]] + shp[axis + 2:])


def _split(full, axis):
    shp = list(full.shape)
    g = full.reshape(shp[:axis] + [N_DEV, shp[axis] // N_DEV] + shp[axis + 1:])
    return jnp.moveaxis(g, axis, 0)


def _pad_lanes(vec):
    return jnp.pad(vec.astype(F32), (0, LANES - vec.shape[0])).reshape(1, LANES)


def _ffn_fwd(x, norm, w_up, conv_w, w_down, tag):
    h = _rmsnorm_fwd(x, norm, f"{tag}_norm")
    u_pre = _matmul(h, w_up, "nn", F32, f"{tag}_up")
    act = _ffn_mid_fwd(u_pre, conv_w, f"{tag}_mid")
    out = _matmul(act, w_down, "nn", F32, f"{tag}_down", residual=x)
    return out, (x, h, u_pre, act)


def _ffn_bwd(saved, norm, w_up, conv_w, w_down, dx, dxb, tag):
    x, h, u_pre, act = saved
    d_w_down = _matmul(act, dxb, "tn", F32, f"{tag}_dwdown")
    dact = _matmul(dxb, w_down, "nt", F32, f"{tag}_dact")
    du = _ffn_mid_bwd(u_pre, conv_w, dact, f"{tag}_dmid")
    du_pre, d_conv = _conv_bwd(du, u_pre, conv_w, f"{tag}_dconv")
    d_w_up = _matmul(h, du_pre, "tn", F32, f"{tag}_dwup")
    dh = _matmul(du_pre, w_up, "nt", F32, f"{tag}_dh")
    dx, dxb, d_norm = _rmsnorm_bwd(x, norm, dh, dx, f"{tag}_dnorm")
    return dx, dxb, dict(norm=d_norm, w_up=d_w_up, conv=d_conv, w_down=d_w_down)


def _gdn_fwd(x, norm, w_in, conv_w, alog, dtb, gain, w_out, H, tag):
    h = _rmsnorm_fwd(x, norm, f"{tag}_norm")
    proj = _matmul(h, w_in, "nn", F32, f"{tag}_in")
    q, k, v, gates = _gdn_pre_fwd(proj, conv_w, alog, dtb, H, f"{tag}_pre")
    o, s_hist = _gdn_rule_fwd(q, k, v, gates, H, f"{tag}_rule")
    y = _gdn_post_fwd(o, proj, gain, f"{tag}_post")
    out = _matmul(y, w_out, "nn", F32, f"{tag}_out", residual=x)
    return out, (x, h, proj, q, k, v, gates, o, s_hist, y)


def _gdn_bwd(saved, norm, w_in, conv_w, alog, dtb, gain, w_out, dx, dxb, H, tag):
    x, h, proj, q, k, v, gates, o, s_hist, y = saved
    D = H * HEAD_DIM
    d_w_out = _matmul(y, dxb, "tn", F32, f"{tag}_dwout")
    dy = _matmul(dxb, w_out, "nt", F32, f"{tag}_dy")
    do, dz, d_gain = _gdn_post_bwd(o, proj, gain, dy, f"{tag}_dpost")
    dq, dk, dv, dgates = _gdn_rule_bwd(q, k, v, gates, s_hist, do, H, f"{tag}_drule")
    dc, dab, d_alog, d_dtb = _gdn_pre_bwd(proj, conv_w, alog, dtb, dq, dk, dv, dgates, H, f"{tag}_dpre")
    dqkv, d_conv = _conv_bwd(dc, proj, conv_w, f"{tag}_dconv")
    dproj = jnp.concatenate([dqkv, dz, dab], axis=1)
    d_w_in = _matmul(h, dproj, "tn", F32, f"{tag}_dwin")
    dh = _matmul(dproj, w_in, "nt", F32, f"{tag}_dh")
    dx, dxb, d_norm = _rmsnorm_bwd(x, norm, dh, dx, f"{tag}_dnorm")
    grads = dict(norm=d_norm, w_in=d_w_in[:, :4 * D + 2 * H], conv=d_conv, alog=d_alog[0, :H], dtb=d_dtb[0, :H],
                 gain=d_gain[0], w_out=d_w_out)
    return dx, dxb, grads


def _sb_fwd(x, norm, w_q, q_gain, w_out, k, v, tag):
    h = _rmsnorm_fwd(x, norm, f"{tag}_norm")
    qp = _matmul(h, w_q, "nn", F32, f"{tag}_q")
    q = _qnorm_fwd(qp, q_gain, f"{tag}_qnorm")
    o = _sb_attn_fwd(q, k, v, f"{tag}_attn")
    out = _matmul(o, w_out, "nn", F32, f"{tag}_out", residual=x)
    return out, (x, h, qp, q, o)


def _sb_bwd(saved, norm, w_q, q_gain, w_out, k, v, dx, dxb, tag):
    x, h, qp, q, o = saved
    d_w_out = _matmul(o, dxb, "tn", F32, f"{tag}_dwout")
    do = _matmul(dxb, w_out, "nt", BF16, f"{tag}_do")
    dq, dk, dv = _sb_attn_bwd(q, k, v, do, f"{tag}_dattn")
    dqp, d_qgain = _qnorm_bwd(qp, q_gain, dq, f"{tag}_dqnorm")
    d_w_q = _matmul(h, dqp, "tn", F32, f"{tag}_dwq")
    dh = _matmul(dqp, w_q, "nt", F32, f"{tag}_dh")
    dx, dxb, d_norm = _rmsnorm_bwd(x, norm, dh, dx, f"{tag}_dnorm")
    return dx, dxb, dk, dv, dict(norm=d_norm, w_q=d_w_q, q_gain=d_qgain[0], w_out=d_w_out)


def _local_step(x, target, W):
    T, D = x.shape
    H = D // HEAD_DIM
    n_a, n_b = W["a_w_in"].shape[0], W["b_w_q"].shape[0]
    row = lambda a: a.reshape(1, -1)
    saved = []
    for l in range(n_a):
        x, s = _gdn_fwd(x, row(W["a_norm"][l]), W["a_w_in"][l], W["a_conv"][l], _pad_lanes(W["a_log"][l]),
                        _pad_lanes(W["a_dt_bias"][l]), row(W["a_out_norm"][l]), W["a_w_out"][l], H, f"gdn{l}")
        saved.append(s)
        x, s = _ffn_fwd(x, row(W["ffn_norm"][l]), W["ffn_w_up"][l], W["ffn_conv"][l], W["ffn_w_down"][l], f"ffn{l}")
        saved.append(s)
    x_kv = x
    h_kv = _rmsnorm_fwd(x_kv, row(W["kv_norm"]), "kv_norm")
    kv = _matmul(h_kv, W["w_kv"], "nn", F32, "kv_proj")
    k, v = _kv_post_fwd(kv, row(W["k_norm"]), "kv_post")
    for j in range(n_b):
        l = n_a + j
        x, s = _sb_fwd(x, row(W["b_norm"][j]), W["b_w_q"][j], row(W["q_norm"][j]), W["b_w_out"][j], k, v, f"sb{j}")
        saved.append(s)
        x, s = _ffn_fwd(x, row(W["ffn_norm"][l]), W["ffn_w_up"][l], W["ffn_conv"][l], W["ffn_w_down"][l], f"ffn{l}")
        saved.append(s)

    loss, dx, dxb = _loss_grad(x, target, "loss")

    G = {name: [None] * W[name].shape[0] for name in
         ("a_norm", "a_w_in", "a_conv", "a_log", "a_dt_bias", "a_out_norm", "a_w_out", "b_norm", "b_w_q", "q_norm", "b_w_out",
          "ffn_norm", "ffn_w_up", "ffn_conv", "ffn_w_down")}

    def take_ffn(l, g):
        G["ffn_norm"][l], G["ffn_w_up"][l], G["ffn_conv"][l], G["ffn_w_down"][l] = g["norm"][0], g["w_up"], g["conv"], g["w_down"]

    dks, dvs = [], []
    for j in reversed(range(n_b)):
        l = n_a + j
        dx, dxb, g = _ffn_bwd(saved.pop(), row(W["ffn_norm"][l]), W["ffn_w_up"][l], W["ffn_conv"][l], W["ffn_w_down"][l],
                              dx, dxb, f"ffn{l}")
        take_ffn(l, g)
        dx, dxb, dk, dv, g = _sb_bwd(saved.pop(), row(W["b_norm"][j]), W["b_w_q"][j], row(W["q_norm"][j]), W["b_w_out"][j],
                                     k, v, dx, dxb, f"sb{j}")
        G["b_norm"][j], G["b_w_q"][j], G["q_norm"][j], G["b_w_out"][j] = g["norm"][0], g["w_q"], g["q_gain"], g["w_out"]
        dks.append(dk)
        dvs.append(dv)
    assert n_b == 2
    dkv, d_k_norm = _kv_post_bwd(kv, row(W["k_norm"]), dks[0], dks[1], dvs[0], dvs[1], "kv_dpost")
    G["w_kv"] = _matmul(h_kv, dkv, "tn", F32, "kv_dw")
    dh_kv = _matmul(dkv, W["w_kv"], "nt", F32, "kv_dh")
    dx, dxb, d_kv_norm = _rmsnorm_bwd(x_kv, row(W["kv_norm"]), dh_kv, dx, "kv_dnorm")
    G["k_norm"], G["kv_norm"] = d_k_norm[0], d_kv_norm[0]
    for l in reversed(range(n_a)):
        dx, dxb, g = _ffn_bwd(saved.pop(), row(W["ffn_norm"][l]), W["ffn_w_up"][l], W["ffn_conv"][l], W["ffn_w_down"][l],
                              dx, dxb, f"ffn{l}")
        take_ffn(l, g)
        dx, dxb, g = _gdn_bwd(saved.pop(), row(W["a_norm"][l]), W["a_w_in"][l], W["a_conv"][l], _pad_lanes(W["a_log"][l]),
                              _pad_lanes(W["a_dt_bias"][l]), row(W["a_out_norm"][l]), W["a_w_out"][l], dx, dxb, H, f"gdn{l}")
        (G["a_norm"][l], G["a_w_in"][l], G["a_conv"][l], G["a_log"][l], G["a_dt_bias"][l], G["a_out_norm"][l],
         G["a_w_out"][l]) = g["norm"][0], g["w_in"], g["conv"], g["alog"], g["dtb"], g["gain"], g["w_out"]
    grads = {name: (jnp.stack(val) if isinstance(val, list) else val) for name, val in G.items()}
    return loss, dx, grads


WEIGHTS = ["a_norm", "a_w_in", "a_conv", "a_log", "a_dt_bias", "a_out_norm", "a_w_out", "kv_norm", "w_kv", "k_norm", "b_norm",
           "b_w_q", "q_norm", "b_w_out", "ffn_norm", "ffn_w_up", "ffn_conv", "ffn_w_down"]
SHARD_AXIS = {"a_norm": 1, "a_w_in": 2, "a_conv": 2, "a_w_out": 1, "w_kv": 1, "b_w_q": 1, "b_w_out": 1, "ffn_w_up": 2,
              "ffn_conv": 2, "ffn_w_down": 1}
MATMUL_WEIGHTS = ["a_w_in", "a_w_out", "w_kv", "b_w_q", "b_w_out", "ffn_w_up", "ffn_w_down"]
VECTOR_WEIGHTS = ["a_norm", "a_conv", "ffn_conv"]
SHARDED = MATMUL_WEIGHTS + VECTOR_WEIGHTS
REPLICATED = [n for n in WEIGHTS if n not in SHARD_AXIS]
ADAM_ROW_TILE = 256


def kernel(x, a_norm, a_w_in, a_conv, a_log, a_dt_bias, a_out_norm, a_w_out, kv_norm, w_kv, k_norm, b_norm, b_w_q, q_norm, b_w_out, ffn_norm, ffn_w_up, ffn_conv, ffn_w_down, loss_target, m_a_norm, m_a_w_in, m_a_conv, m_a_log, m_a_dt_bias, m_a_out_norm, m_a_w_out, m_kv_norm, m_w_kv, m_k_norm, m_b_norm, m_b_w_q, m_q_norm, m_b_w_out, m_ffn_norm, m_ffn_w_up, m_ffn_conv, m_ffn_w_down, v_a_norm, v_a_w_in, v_a_conv, v_a_log, v_a_dt_bias, v_a_out_norm, v_a_w_out, v_kv_norm, v_w_kv, v_k_norm, v_b_norm, v_b_w_q, v_q_norm, v_b_w_out, v_ffn_norm, v_ffn_w_up, v_ffn_conv, v_ffn_w_down):
    w = dict(a_norm=a_norm, a_w_in=a_w_in, a_conv=a_conv, a_log=a_log, a_dt_bias=a_dt_bias, a_out_norm=a_out_norm, a_w_out=a_w_out,
             kv_norm=kv_norm, w_kv=w_kv, k_norm=k_norm, b_norm=b_norm, b_w_q=b_w_q, q_norm=q_norm, b_w_out=b_w_out,
             ffn_norm=ffn_norm, ffn_w_up=ffn_w_up, ffn_conv=ffn_conv, ffn_w_down=ffn_w_down)
    m = dict(a_norm=m_a_norm, a_w_in=m_a_w_in, a_conv=m_a_conv, a_log=m_a_log, a_dt_bias=m_a_dt_bias, a_out_norm=m_a_out_norm,
             a_w_out=m_a_w_out, kv_norm=m_kv_norm, w_kv=m_w_kv, k_norm=m_k_norm, b_norm=m_b_norm, b_w_q=m_b_w_q, q_norm=m_q_norm,
             b_w_out=m_b_w_out, ffn_norm=m_ffn_norm, ffn_w_up=m_ffn_w_up, ffn_conv=m_ffn_conv, ffn_w_down=m_ffn_w_down)
    v = dict(a_norm=v_a_norm, a_w_in=v_a_w_in, a_conv=v_a_conv, a_log=v_a_log, a_dt_bias=v_a_dt_bias, a_out_norm=v_a_out_norm,
             a_w_out=v_a_w_out, kv_norm=v_kv_norm, w_kv=v_w_kv, k_norm=v_k_norm, b_norm=v_b_norm, b_w_q=v_b_w_q, q_norm=v_q_norm,
             b_w_out=v_b_w_out, ffn_norm=v_ffn_norm, ffn_w_up=v_ffn_w_up, ffn_conv=v_ffn_conv, ffn_w_down=v_ffn_w_down)
    D = x.shape[-1]
    H = D // HEAD_DIM

    mat_shapes = [w[n].shape for n in MATMUL_WEIGHTS]
    vec_shapes = [w[n].shape for n in VECTOR_WEIGHTS]
    mats = _exchange(_pack([w[n] for n in MATMUL_WEIGHTS], BF16, PACK_ROW_ALIGN), True, "gather_matmul_weights")
    vecs = _exchange(_pack([w[n] for n in VECTOR_WEIGHTS], F32, SUBLANES), True, "gather_vector_weights")
    W = {n: w[n] for n in REPLICATED}
    for n, blocks in zip(MATMUL_WEIGHTS, _unpack(mats, mat_shapes, PACK_ROW_ALIGN)):
        W[n] = _join(blocks, SHARD_AXIS[n])
    for n, blocks in zip(VECTOR_WEIGHTS, _unpack(vecs, vec_shapes, SUBLANES)):
        W[n] = _join(blocks, SHARD_AXIS[n])
    W["a_w_in"] = jnp.pad(W["a_w_in"], ((0, 0), (0, 0), (0, 4 * D + LANES - W["a_w_in"].shape[2])))

    loss, grad_x, G = _local_step(x[0], loss_target[0], W)

    send = _pack([_split(G[n], SHARD_AXIS[n]) for n in SHARDED], F32, SUBLANES, ADAM_ROW_TILE, lead=1)
    parts = _exchange(send, False, "scatter_gradients")
    shard_shapes = [w[n].shape for n in SHARDED]
    pk = lambda d: _pack([d[n] for n in SHARDED], F32, SUBLANES, ADAM_ROW_TILE)
    outs = _adamw(parts, pk(w), pk(m), pk(v), "adamw_shards")
    big = [dict(zip(SHARDED, _unpack(o, shard_shapes, SUBLANES))) for o in outs]

    rep_shapes = [w[n].shape for n in REPLICATED]
    rp = lambda d: _pack([d[n] for n in REPLICATED], F32, SUBLANES, SUBLANES)
    rep_parts = _exchange(rp(G), True, "gather_replicated_gradients")
    outs = _adamw(rep_parts, rp(w), rp(m), rp(v), "adamw_replicated")
    small = [dict(zip(REPLICATED, _unpack(o, rep_shapes, SUBLANES))) for o in outs]

    total_loss = lax.psum(loss[0, 0], ("x", "y", "c"))
    result = [total_loss, grad_x[None]]
    for kind in range(4):
        result += [big[kind][n] if n in SHARD_AXIS else small[kind][n] for n in WEIGHTS]
    return tuple(result)
```

```python
import functools
import math

import jax
import jax.numpy as jnp
from jax import lax
from jax.experimental import pallas as pl
from jax.experimental.pallas import tpu as pltpu

F32 = jnp.float32
BF16 = jnp.bfloat16
X3 = "x3"
LHS01 = "lhs01"
MESH = pl.DeviceIdType.MESH

N_DEV = 8
LANES = 128
SUBLANES = 8
PACK_COLS = 1024
PACK_ROW_ALIGN = 16
HEAD_DIM = 128
GDN_CHUNK = 128
EPS = 1e-6
ATTN_TQ = 256
ATTN_TK = 128
EXP_UNDERFLOW = 104.0
VMEM_LIMIT = 48 * 1024 * 1024
MATMUL_TILE = 1408

ADAM_LR = 0.001
ADAM_B1 = 0.9
ADAM_B2 = 0.999
ADAM_EPS = 1e-08
ADAM_WD = 0.01
ADAM_STEP = 10


def _tile(n, pref):
    if n <= pref:
        return n
    best = 0
    for t in range(LANES, pref + 1, LANES):
        if n % t == 0:
            best = t
    assert best, (n, pref)
    return best


def _params(*sem):
    return pltpu.CompilerParams(dimension_semantics=tuple(sem), vmem_limit_bytes=VMEM_LIMIT)


_DIMS = {"nn": (((1,), (0,)), ((), ())), "nt": (((1,), (1,)), ((), ())), "tn": (((0,), (0,)), ((), ()))}


def _split3(x):
    hi = x.astype(BF16)
    r1 = x - hi.astype(F32)
    mid = r1.astype(BF16)
    lo = (r1 - mid.astype(F32)).astype(BF16)
    return hi, mid, lo


def _raw_dot(mode, prec, a, b):
    if prec is None:
        return lax.dot_general(a, b, _DIMS[mode], preferred_element_type=F32)
    if prec == LHS01:
        b_hi, b_mid, b_lo = _split3(b)
        a_hi = a.astype(BF16)
        terms = [(a_hi, b_lo), (a_hi, b_mid), (a_hi, b_hi)]
    else:
        (a_hi, a_mid, _), (b_hi, b_mid, _) = _split3(a), _split3(b)
        terms = [(a_mid, b_hi), (a_hi, b_mid), (a_hi, b_hi)]
    out = None
    for ta, tb in terms:
        term = lax.dot_general(ta, tb, _DIMS[mode], preferred_element_type=F32)
        out = term if out is None else out + term
    return out


@functools.partial(jax.custom_vjp, nondiff_argnums=(0, 1))
def _dot(mode, prec, a, b):
    return _raw_dot(mode, prec, a, b)


def _dot_fwd(mode, prec, a, b):
    return _raw_dot(mode, prec, a, b), (a, b)


def _dot_bwd(mode, prec, res, ct):
    a, b = res
    if prec == LHS01:
        assert mode == "nn"
        return jnp.zeros_like(a), _dot("tn", LHS01, a, ct)
    if mode == "nn":
        return _dot("nt", prec, ct, b), _dot("tn", prec, a, ct)
    if mode == "nt":
        return _dot("nn", prec, ct, b), _dot("tn", prec, ct, a)
    return _dot("nt", prec, b, ct), _dot("nn", prec, a, ct)


_dot.defvjp(_dot_fwd, _dot_bwd)


def _softplus(x):
    return jnp.maximum(x, 0.0) + jnp.log(1.0 + jnp.exp(-jnp.abs(x)))


def _matmul(a, b, mode, out_dtype, name, residual=None):
    if mode == "nn":
        (M, K), N = a.shape, b.shape[1]
    elif mode == "nt":
        (M, K), N = a.shape, b.shape[0]
    else:
        (K, M), N = a.shape, b.shape[1]
    tm, tn, tk = _tile(M, MATMUL_TILE), _tile(N, MATMUL_TILE), _tile(K, MATMUL_TILE)
    nk = K // tk

    def body(*refs):
        if residual is None:
            a_ref, b_ref, o_ref, acc_ref = refs
        else:
            a_ref, b_ref, r_ref, o_ref, acc_ref = refs

        def finish(r):
            if residual is not None:
                r = r + r_ref[...]
            o_ref[...] = r.astype(out_dtype)

        if nk == 1:
            finish(_raw_dot(mode, None, a_ref[...], b_ref[...]))
            return
        k = pl.program_id(2)

        @pl.when(k == 0)
        def _():
            acc_ref[...] = jnp.zeros_like(acc_ref)

        acc_ref[...] += _raw_dot(mode, None, a_ref[...], b_ref[...])

        @pl.when(k == nk - 1)
        def _():
            finish(acc_ref[...])

    a_spec = pl.BlockSpec((tk, tm), lambda i, j, k: (k, i)) if mode == "tn" else pl.BlockSpec((tm, tk), lambda i, j, k: (i, k))
    b_spec = pl.BlockSpec((tn, tk), lambda i, j, k: (j, k)) if mode == "nt" else pl.BlockSpec((tk, tn), lambda i, j, k: (k, j))
    o_spec = pl.BlockSpec((tm, tn), lambda i, j, k: (i, j))
    in_specs, args = [a_spec, b_spec], [a, b]
    if residual is not None:
        in_specs.append(o_spec)
        args.append(residual)
    return pl.pallas_call(
        body, name=name, grid=(M // tm, N // tn, nk), in_specs=in_specs, out_specs=o_spec,
        out_shape=jax.ShapeDtypeStruct((M, N), out_dtype),
        scratch_shapes=[pltpu.VMEM((tm, tn) if nk > 1 else (SUBLANES, LANES), F32)],
        compiler_params=_params("parallel", "parallel", "arbitrary"),
    )(*args)


def _rmsnorm_fwd(x, g, name):
    T, D = x.shape
    tt = _tile(T, 512)

    def body(x_ref, g_ref, o_ref):
        xv = x_ref[...]
        r = lax.rsqrt(jnp.mean(xv * xv, axis=-1, keepdims=True) + EPS)
        o_ref[...] = (xv * r * g_ref[...]).astype(BF16)

    row = pl.BlockSpec((tt, D), lambda i: (i, 0))
    return pl.pallas_call(
        body, name=name, grid=(T // tt,), in_specs=[row, pl.BlockSpec((1, D), lambda i: (0, 0))], out_specs=row,
        out_shape=jax.ShapeDtypeStruct((T, D), BF16), compiler_params=_params("parallel"),
    )(x, g)


def _rmsnorm_bwd(x, g, dh, dx_in, name):
    T, D = x.shape
    tt = _tile(T, 512)

    def body(x_ref, g_ref, dh_ref, dxin_ref, dx_ref, dxb_ref, dg_ref):
        xv = x_ref[...]
        dhv = dh_ref[...].astype(F32)
        r = lax.rsqrt(jnp.mean(xv * xv, axis=-1, keepdims=True) + EPS)
        dn = dhv * g_ref[...]
        dx = r * dn - xv * (r * r * r) * jnp.mean(dn * xv, axis=-1, keepdims=True) + dxin_ref[...]
        dx_ref[...] = dx
        dxb_ref[...] = dx.astype(BF16)

        @pl.when(pl.program_id(0) == 0)
        def _():
            dg_ref[...] = jnp.zeros_like(dg_ref)

        dg_ref[...] += jnp.sum(dhv * xv * r, axis=0, keepdims=True)

    row = pl.BlockSpec((tt, D), lambda i: (i, 0))
    vec = pl.BlockSpec((1, D), lambda i: (0, 0))
    return pl.pallas_call(
        body, name=name, grid=(T // tt,), in_specs=[row, vec, row, row], out_specs=[row, row, vec],
        out_shape=[jax.ShapeDtypeStruct((T, D), F32), jax.ShapeDtypeStruct((T, D), BF16), jax.ShapeDtypeStruct((1, D), F32)],
        compiler_params=_params("arbitrary"),
    )(x, g, dh, dx_in)


def _headnorm(x, gain):
    outs = []
    for h in range(x.shape[1] // HEAD_DIM):
        xh = x[:, h * HEAD_DIM:(h + 1) * HEAD_DIM]
        outs.append(xh * lax.rsqrt(jnp.mean(xh * xh, axis=-1, keepdims=True) + EPS) * gain)
    return jnp.concatenate(outs, axis=1)


def _qnorm_fwd(qp, gain, name):
    T, D = qp.shape
    tt = _tile(T, 512)

    def body(x_ref, g_ref, o_ref):
        o_ref[...] = _headnorm(x_ref[...], g_ref[...]).astype(BF16)

    row = pl.BlockSpec((tt, D), lambda i: (i, 0))
    return pl.pallas_call(
        body, name=name, grid=(T // tt,), in_specs=[row, pl.BlockSpec((1, HEAD_DIM), lambda i: (0, 0))], out_specs=row,
        out_shape=jax.ShapeDtypeStruct((T, D), BF16), compiler_params=_params("parallel"),
    )(qp, gain)


def _qnorm_bwd(qp, gain, dq, name):
    T, D = qp.shape
    tt = _tile(T, 512)

    def body(x_ref, g_ref, dq_ref, dx_ref, dg_ref):
        _, vjp = jax.vjp(_headnorm, x_ref[...], g_ref[...])
        dx, dg = vjp(dq_ref[...])
        dx_ref[...] = dx.astype(BF16)

        @pl.when(pl.program_id(0) == 0)
        def _():
            dg_ref[...] = jnp.zeros_like(dg_ref)

        dg_ref[...] += dg

    row = pl.BlockSpec((tt, D), lambda i: (i, 0))
    vec = pl.BlockSpec((1, HEAD_DIM), lambda i: (0, 0))
    return pl.pallas_call(
        body, name=name, grid=(T // tt,), in_specs=[row, vec, row], out_specs=[row, vec],
        out_shape=[jax.ShapeDtypeStruct((T, D), BF16), jax.ShapeDtypeStruct((1, HEAD_DIM), F32)],
        compiler_params=_params("arbitrary"),
    )(qp, gain, dq)


def _kv_post_fwd(kv, gain, name):
    T, D2 = kv.shape
    D = D2 // 2
    tt = _tile(T, 512)

    def body(k_ref, v_ref, g_ref, ko_ref, vo_ref):
        ko_ref[...] = _headnorm(k_ref[...], g_ref[...]).astype(BF16)
        vo_ref[...] = v_ref[...].astype(BF16)

    row = pl.BlockSpec((tt, D), lambda i: (i, 0))
    return pl.pallas_call(
        body, name=name, grid=(T // tt,),
        in_specs=[row, pl.BlockSpec((tt, D), lambda i: (i, 1)), pl.BlockSpec((1, HEAD_DIM), lambda i: (0, 0))],
        out_specs=[row, row], out_shape=[jax.ShapeDtypeStruct((T, D), BF16)] * 2, compiler_params=_params("parallel"),
    )(kv, kv, gain)


def _kv_post_bwd(kv, gain, dk_a, dk_b, dv_a, dv_b, name):
    T, D2 = kv.shape
    D = D2 // 2
    tt = _tile(T, 512)

    def body(k_ref, g_ref, dka_ref, dkb_ref, dva_ref, dvb_ref, dkv_ref, dg_ref):
        _, vjp = jax.vjp(_headnorm, k_ref[...], g_ref[...])
        dx, dg = vjp(dka_ref[...] + dkb_ref[...])
        dkv_ref[:, :D] = dx.astype(BF16)
        dkv_ref[:, D:] = (dva_ref[...] + dvb_ref[...]).astype(BF16)

        @pl.when(pl.program_id(0) == 0)
        def _():
            dg_ref[...] = jnp.zeros_like(dg_ref)

        dg_ref[...] += dg

    row = pl.BlockSpec((tt, D), lambda i: (i, 0))
    vec = pl.BlockSpec((1, HEAD_DIM), lambda i: (0, 0))
    return pl.pallas_call(
        body, name=name, grid=(T // tt,), in_specs=[row, vec, row, row, row, row],
        out_specs=[pl.BlockSpec((tt, D2), lambda i: (i, 0)), vec],
        out_shape=[jax.ShapeDtypeStruct((T, D2), BF16), jax.ShapeDtypeStruct((1, HEAD_DIM), F32)],
        compiler_params=_params("arbitrary"),
    )(kv, gain, dk_a, dk_b, dv_a, dv_b)


def _loss_grad(y, target, name):
    T, D = y.shape
    tt = _tile(T, 512)
    nt = T // tt

    def body(y_ref, t_ref, loss_ref, dy_ref, dyb_ref, acc_ref):
        i = pl.program_id(0)
        d = y_ref[...] - t_ref[...]
        dy = d * (1.0 / D)
        dy_ref[...] = dy
        dyb_ref[...] = dy.astype(BF16)

        @pl.when(i == 0)
        def _():
            acc_ref[...] = jnp.zeros_like(acc_ref)

        acc_ref[...] += jnp.sum(d * d, axis=0, keepdims=True)

        @pl.when(i == nt - 1)
        def _():
            loss_ref[...] = jnp.zeros_like(loss_ref) + jnp.sum(acc_ref[...]) * (0.5 / D)

    row = pl.BlockSpec((tt, D), lambda i: (i, 0))
    return pl.pallas_call(
        body, name=name, grid=(nt,), in_specs=[row, row], out_specs=[pl.BlockSpec((1, LANES), lambda i: (0, 0)), row, row],
        out_shape=[jax.ShapeDtypeStruct((1, LANES), F32), jax.ShapeDtypeStruct((T, D), F32), jax.ShapeDtypeStruct((T, D), BF16)],
        scratch_shapes=[pltpu.VMEM((1, D), F32)], compiler_params=_params("arbitrary"),
    )(y, target)


def _conv_tile(halo, main, w):
    K = w.shape[0]
    xc = jnp.concatenate([halo, main], axis=0)
    y = main * w[K - 1:K, :]
    for k in range(K - 1):
        y = y + pltpu.roll(xc, K - 1 - k, 0)[SUBLANES:, :] * w[k:k + 1, :]
    return y


def _prev_halo(tt, col):
    return lambda cb: pl.BlockSpec((SUBLANES, cb), lambda j, i: (jnp.maximum(i * (tt // SUBLANES) - 1, 0), col(j)))


def _conv_bwd(dy, x, w, name):
    T, C = dy.shape
    K = w.shape[0]
    tt, cb = _tile(T, 256), _tile(C, 1024)
    nt = T // tt

    def body(dy_ref, dyn_ref, x_ref, xp_ref, w_ref, dx_ref, dw_ref):
        i = pl.program_id(1)
        wv = w_ref[...]
        dyv = dy_ref[...]
        nxt = jnp.where(i == nt - 1, 0.0, dyn_ref[...])
        prev = jnp.where(i == 0, 0.0, xp_ref[...])
        dyc = jnp.concatenate([dyv, nxt], axis=0)
        xc = jnp.concatenate([prev, x_ref[...]], axis=0)
        tap = lax.broadcasted_iota(jnp.int32, (K, cb), 0)
        dx = dyv * wv[K - 1:K, :]
        dw = jnp.where(tap == K - 1, jnp.sum(dyv * x_ref[...], axis=0, keepdims=True), 0.0)
        for k in range(K - 1):
            s = K - 1 - k
            dx = dx + pltpu.roll(dyc, SUBLANES - s, 0)[SUBLANES:, :] * wv[k:k + 1, :]
            dw = dw + jnp.where(tap == k, jnp.sum(dyv * pltpu.roll(xc, s, 0)[SUBLANES:, :], axis=0, keepdims=True), 0.0)
        dx_ref[...] = dx.astype(BF16)

        @pl.when(i == 0)
        def _():
            dw_ref[...] = jnp.zeros_like(dw_ref)

        dw_ref[...] += dw

    main = pl.BlockSpec((tt, cb), lambda j, i: (i, j))
    nxt = pl.BlockSpec((SUBLANES, cb), lambda j, i: (jnp.minimum((i + 1) * (tt // SUBLANES), T // SUBLANES - 1), j))
    wspec = pl.BlockSpec((K, cb), lambda j, i: (0, j))
    return pl.pallas_call(
        body, name=name, grid=(C // cb, nt), in_specs=[main, nxt, main, _prev_halo(tt, lambda j: j)(cb), wspec],
        out_specs=[main, wspec], out_shape=[jax.ShapeDtypeStruct((T, C), BF16), jax.ShapeDtypeStruct((K, C), F32)],
        compiler_params=_params("parallel", "arbitrary"),
    )(dy, dy, x, x, w)


def _ffn_mid_specs(T, F):
    tt, cb = _tile(T, 256), _tile(F, 512)
    nf = F // cb
    gate = pl.BlockSpec((tt, cb), lambda j, i: (i, j % nf))
    up = pl.BlockSpec((tt, cb), lambda j, i: (i, nf + j % nf))
    gate_h = _prev_halo(tt, lambda j: j % nf)(cb)
    up_h = _prev_halo(tt, lambda j: nf + j % nf)(cb)
    wg = pl.BlockSpec((3, cb), lambda j, i: (0, j % nf))
    wu = pl.BlockSpec((3, cb), lambda j, i: (0, nf + j % nf))
    return tt, cb, nf, [gate, gate_h, up, up_h, wg, wu]


def _ffn_mid_fwd(u_pre, w, name):
    T, F2 = u_pre.shape
    F = F2 // 2
    tt, cb, nf, specs = _ffn_mid_specs(T, F)

    def body(g_ref, gh_ref, u_ref, uh_ref, wg_ref, wu_ref, o_ref):
        first = pl.program_id(1) == 0
        ug = _conv_tile(jnp.where(first, 0.0, gh_ref[...]), g_ref[...], wg_ref[...])
        uu = _conv_tile(jnp.where(first, 0.0, uh_ref[...]), u_ref[...], wu_ref[...])
        o_ref[...] = (ug * jax.nn.sigmoid(ug) * uu).astype(BF16)

    return pl.pallas_call(
        body, name=name, grid=(nf, T // tt), in_specs=specs, out_specs=pl.BlockSpec((tt, cb), lambda j, i: (i, j)),
        out_shape=jax.ShapeDtypeStruct((T, F), BF16), compiler_params=_params("parallel", "parallel"),
    )(u_pre, u_pre, u_pre, u_pre, w, w)


def _ffn_mid_bwd(u_pre, w, dact, name):
    T, F2 = u_pre.shape
    F = F2 // 2
    tt, cb, nf, specs = _ffn_mid_specs(T, F)

    def body(g_ref, gh_ref, u_ref, uh_ref, wg_ref, wu_ref, da_ref, o_ref):
        first = pl.program_id(1) == 0
        ug = _conv_tile(jnp.where(first, 0.0, gh_ref[...]), g_ref[...], wg_ref[...])
        uu = _conv_tile(jnp.where(first, 0.0, uh_ref[...]), u_ref[...], wu_ref[...])
        sg = jax.nn.sigmoid(ug)
        da = da_ref[...]
        d_gate = da * uu * sg * (1.0 + ug * (1.0 - sg))
        d_up = da * ug * sg
        o_ref[...] = jnp.where(pl.program_id(0) < nf, d_gate, d_up)

    return pl.pallas_call(
        body, name=name, grid=(2 * nf, T // tt), in_specs=specs + [pl.BlockSpec((tt, cb), lambda j, i: (i, j % nf))],
        out_specs=pl.BlockSpec((tt, cb), lambda j, i: (i, j)), out_shape=jax.ShapeDtypeStruct((T, F2), F32),
        compiler_params=_params("parallel", "parallel"),
    )(u_pre, u_pre, u_pre, u_pre, w, w, dact)


def _gdn_act(c, ab, alog, dtb, H):
    D = H * HEAD_DIM
    s = c * jax.nn.sigmoid(c)
    qs, ks = [], []
    for h in range(H):
        qh = s[:, h * HEAD_DIM:(h + 1) * HEAD_DIM]
        kh = s[:, D + h * HEAD_DIM:D + (h + 1) * HEAD_DIM]
        qs.append(qh * lax.rsqrt(jnp.sum(qh * qh, axis=-1, keepdims=True) + EPS) * (HEAD_DIM ** -0.5))
        ks.append(kh * lax.rsqrt(jnp.sum(kh * kh, axis=-1, keepdims=True) + EPS))
    g = -jnp.exp(alog) * _softplus(ab + dtb)
    lane = lax.broadcasted_iota(jnp.int32, ab.shape, 1)
    gates = jnp.where(lane < H, g, jax.nn.sigmoid(ab))
    return jnp.concatenate(qs, axis=1), jnp.concatenate(ks, axis=1), s[:, 2 * D:], gates


def _gdn_pre_specs(T, D):
    tt = _tile(T, 256)
    main = pl.BlockSpec((tt, 3 * D), lambda i: (i, 0))
    halo = pl.BlockSpec((SUBLANES, 3 * D), lambda i: (jnp.maximum(i * (tt // SUBLANES) - 1, 0), 0))
    ab = pl.BlockSpec((tt, LANES), lambda i: (i, 4 * D // LANES))
    return tt, main, halo, ab


def _gdn_pre_fwd(proj, conv_w, alog, dtb, H, name):
    T = proj.shape[0]
    D = H * HEAD_DIM
    tt, main, halo, ab = _gdn_pre_specs(T, D)
    K = conv_w.shape[0]

    def body(x_ref, xh_ref, ab_ref, w_ref, al_ref, dt_ref, q_ref, k_ref, v_ref, g_ref):
        c = _conv_tile(jnp.where(pl.program_id(0) == 0, 0.0, xh_ref[...]), x_ref[...], w_ref[...])
        q, k, v, gates = _gdn_act(c, ab_ref[...], al_ref[...], dt_ref[...], H)
        q_ref[...] = q
        k_ref[...] = k
        v_ref[...] = v
        g_ref[...] = gates

    row = pl.BlockSpec((tt, D), lambda i: (i, 0))
    vec = pl.BlockSpec((1, LANES), lambda i: (0, 0))
    gspec = pl.BlockSpec((tt, LANES), lambda i: (i, 0))
    return pl.pallas_call(
        body, name=name, grid=(T // tt,),
        in_specs=[main, halo, ab, pl.BlockSpec((K, 3 * D), lambda i: (0, 0)), vec, vec], out_specs=[row, row, row, gspec],
        out_shape=[jax.ShapeDtypeStruct((T, D), F32)] * 3 + [jax.ShapeDtypeStruct((T, LANES), F32)],
        compiler_params=_params("parallel"),
    )(proj, proj, proj, conv_w, alog, dtb)


def _gdn_pre_bwd(proj, conv_w, alog, dtb, dq, dk, dv, dgates, H, name):
    T = proj.shape[0]
    D = H * HEAD_DIM
    tt, main, halo, ab = _gdn_pre_specs(T, D)
    K = conv_w.shape[0]

    def body(x_ref, xh_ref, ab_ref, w_ref, al_ref, dt_ref, dq_ref, dk_ref, dv_ref, dg_ref, dc_ref, dab_ref, dal_ref, ddt_ref):
        c = _conv_tile(jnp.where(pl.program_id(0) == 0, 0.0, xh_ref[...]), x_ref[...], w_ref[...])
        _, vjp = jax.vjp(functools.partial(_gdn_act, H=H), c, ab_ref[...], al_ref[...], dt_ref[...])
        dc, dab, dal, ddt = vjp((dq_ref[...], dk_ref[...], dv_ref[...], dg_ref[...]))
        dc_ref[...] = dc
        dab_ref[...] = dab.astype(BF16)

        @pl.when(pl.program_id(0) == 0)
        def _():
            dal_ref[...] = jnp.zeros_like(dal_ref)
            ddt_ref[...] = jnp.zeros_like(ddt_ref)

        dal_ref[...] += dal
        ddt_ref[...] += ddt

    row = pl.BlockSpec((tt, D), lambda i: (i, 0))
    vec = pl.BlockSpec((1, LANES), lambda i: (0, 0))
    gspec = pl.BlockSpec((tt, LANES), lambda i: (i, 0))
    return pl.pallas_call(
        body, name=name, grid=(T // tt,),
        in_specs=[main, halo, ab, pl.BlockSpec((K, 3 * D), lambda i: (0, 0)), vec, vec, row, row, row, gspec],
        out_specs=[main, gspec, vec, vec],
        out_shape=[jax.ShapeDtypeStruct((T, 3 * D), F32), jax.ShapeDtypeStruct((T, LANES), BF16),
                   jax.ShapeDtypeStruct((1, LANES), F32), jax.ShapeDtypeStruct((1, LANES), F32)],
        compiler_params=_params("arbitrary"),
    )(proj, proj, proj, conv_w, alog, dtb, dq, dk, dv, dgates)


def _chunk_masks():
    C = GDN_CHUNK
    i = lax.broadcasted_iota(jnp.int32, (C, C), 0)
    j = lax.broadcasted_iota(jnp.int32, (C, C), 1)
    levels = []
    for lv in range(int(math.log2(C))):
        levels.append((lax.shift_right_logical(i, lv + 1) == lax.shift_right_logical(j, lv + 1))
                      & ((lax.shift_right_logical(i, lv) & 1) == 1) & ((lax.shift_right_logical(j, lv) & 1) == 0))
    return dict(tri=i >= j, strict=i > j, eye=i == j, levels=levels,
                lane=lax.broadcasted_iota(jnp.int32, (C, LANES), 1), last_row=lax.broadcasted_iota(jnp.int32, (C, 1), 0) == C - 1)


@jax.custom_vjp
def _unit_lower_inverse(L):
    m = _chunk_masks()
    A = m["eye"].astype(F32)
    for lvl in m["levels"]:
        A = A - _raw_dot("nn", X3, _raw_dot("nn", X3, A, jnp.where(lvl, L, 0.0)), A)
    return A


def _unit_lower_inverse_fwd(L):
    A = _unit_lower_inverse(L)
    return A, A


def _unit_lower_inverse_bwd(A, ct):
    return (-_raw_dot("nt", X3, _raw_dot("tn", X3, A, ct), A),)


_unit_lower_inverse.defvjp(_unit_lower_inverse_fwd, _unit_lower_inverse_bwd)


def _gdn_chunks(q, k, v, gates, S, H, m):
    C = GDN_CHUNK
    tri_f = m["tri"].astype(F32)
    eye_f = m["eye"].astype(F32)
    gam_all = _dot("nn", LHS01, tri_f, gates)
    outs, states = [], []
    for h in range(H):
        sl = slice(h * HEAD_DIM, (h + 1) * HEAD_DIM)
        qh, kh, vh, Sh = q[:, sl], k[:, sl], v[:, sl], S[:, sl]
        gam = jnp.sum(jnp.where(m["lane"] == h, gam_all, 0.0), axis=1, keepdims=True)
        beta = jnp.sum(jnp.where(m["lane"] == H + h, gates, 0.0), axis=1, keepdims=True)
        gam_row = _dot("nn", LHS01, jnp.ones((C, C), F32), jnp.where(m["eye"], gam, 0.0))
        decay = jnp.where(m["tri"], jnp.exp(jnp.where(m["tri"], gam - gam_row, 0.0)), 0.0)
        kb = kh * beta
        L = jnp.where(m["strict"], _dot("nt", None, kb, kh) * decay, 0.0)
        A = _unit_lower_inverse(L)
        eg = jnp.exp(gam)
        u = _dot("nn", X3, A, vh * beta)
        w = _dot("nn", X3, A, kb * eg)
        qk = jnp.where(m["tri"], _dot("nt", None, qh, kh) * decay, 0.0)
        g_last = jnp.sum(jnp.where(m["last_row"], gam, 0.0), axis=0, keepdims=True)
        v_new = u - _dot("nn", None, w, Sh)
        outs.append(_dot("nn", None, qh * eg, Sh) + _dot("nn", None, qk, v_new))
        states.append(Sh * jnp.exp(g_last) + _dot("tn", None, kh * jnp.exp(g_last - gam), v_new))
    return jnp.concatenate(outs, axis=1), jnp.concatenate(states, axis=1)


def _gdn_rule_fwd(q, k, v, gates, H, name):
    T, D = q.shape
    C = GDN_CHUNK

    def body(q_ref, k_ref, v_ref, g_ref, o_ref, sh_ref, s_ref):
        @pl.when(pl.program_id(0) == 0)
        def _():
            s_ref[...] = jnp.zeros_like(s_ref)

        S = s_ref[...]
        sh_ref[...] = S
        o, S_new = _gdn_chunks(q_ref[...], k_ref[...], v_ref[...], g_ref[...], S, H, _chunk_masks())
        o_ref[...] = o
        s_ref[...] = S_new

    row = pl.BlockSpec((C, D), lambda n: (n, 0))
    return pl.pallas_call(
        body, name=name, grid=(T // C,), in_specs=[row, row, row, pl.BlockSpec((C, LANES), lambda n: (n, 0))],
        out_specs=[row, pl.BlockSpec((None, HEAD_DIM, D), lambda n: (n, 0, 0))],
        out_shape=[jax.ShapeDtypeStruct((T, D), F32), jax.ShapeDtypeStruct((T // C, HEAD_DIM, D), F32)],
        scratch_shapes=[pltpu.VMEM((HEAD_DIM, D), F32)], compiler_params=_params("arbitrary"),
    )(q, k, v, gates)


def _gdn_rule_bwd(q, k, v, gates, s_hist, do, H, name):
    T, D = q.shape
    C = GDN_CHUNK
    N = T // C

    def body(q_ref, k_ref, v_ref, g_ref, s_ref, do_ref, dq_ref, dk_ref, dv_ref, dg_ref, ds_ref):
        @pl.when(pl.program_id(0) == 0)
        def _():
            ds_ref[...] = jnp.zeros_like(ds_ref)

        fn = functools.partial(_gdn_chunks, H=H, m=_chunk_masks())
        _, vjp = jax.vjp(fn, q_ref[...], k_ref[...], v_ref[...], g_ref[...], s_ref[...])
        dq, dk, dv, dg, ds = vjp((do_ref[...], ds_ref[...]))
        dq_ref[...] = dq
        dk_ref[...] = dk
        dv_ref[...] = dv
        dg_ref[...] = dg
        ds_ref[...] = ds

    row = pl.BlockSpec((C, D), lambda n: (N - 1 - n, 0))
    gspec = pl.BlockSpec((C, LANES), lambda n: (N - 1 - n, 0))
    return pl.pallas_call(
        body, name=name, grid=(N,),
        in_specs=[row, row, row, gspec, pl.BlockSpec((None, HEAD_DIM, D), lambda n: (N - 1 - n, 0, 0)), row],
        out_specs=[row, row, row, gspec],
        out_shape=[jax.ShapeDtypeStruct((T, D), F32)] * 3 + [jax.ShapeDtypeStruct((T, LANES), F32)],
        scratch_shapes=[pltpu.VMEM((HEAD_DIM, D), F32)], compiler_params=_params("arbitrary"),
    )(q, k, v, gates, s_hist, do)


def _gdn_gate(o, z, gain):
    return _headnorm(o, gain) * (z * jax.nn.sigmoid(z))


def _gdn_post_fwd(o, proj, gain, name):
    T, D = o.shape
    tt = _tile(T, 512)

    def body(o_ref, z_ref, g_ref, y_ref):
        y_ref[...] = _gdn_gate(o_ref[...], z_ref[...], g_ref[...]).astype(BF16)

    row = pl.BlockSpec((tt, D), lambda i: (i, 0))
    return pl.pallas_call(
        body, name=name, grid=(T // tt,),
        in_specs=[row, pl.BlockSpec((tt, D), lambda i: (i, 3)), pl.BlockSpec((1, HEAD_DIM), lambda i: (0, 0))], out_specs=row,
        out_shape=jax.ShapeDtypeStruct((T, D), BF16), compiler_params=_params("parallel"),
    )(o, proj, gain)


def _gdn_post_bwd(o, proj, gain, dy, name):
    T, D = o.shape
    tt = _tile(T, 512)

    def body(o_ref, z_ref, g_ref, dy_ref, do_ref, dz_ref, dg_ref):
        _, vjp = jax.vjp(_gdn_gate, o_ref[...], z_ref[...], g_ref[...])
        do, dz, dg = vjp(dy_ref[...])
        do_ref[...] = do
        dz_ref[...] = dz.astype(BF16)

        @pl.when(pl.program_id(0) == 0)
        def _():
            dg_ref[...] = jnp.zeros_like(dg_ref)

        dg_ref[...] += dg

    row = pl.BlockSpec((tt, D), lambda i: (i, 0))
    vec = pl.BlockSpec((1, HEAD_DIM), lambda i: (0, 0))
    return pl.pallas_call(
        body, name=name, grid=(T // tt,), in_specs=[row, pl.BlockSpec((tt, D), lambda i: (i, 3)), vec, row],
        out_specs=[row, row, vec],
        out_shape=[jax.ShapeDtypeStruct((T, D), F32), jax.ShapeDtypeStruct((T, D), BF16), jax.ShapeDtypeStruct((1, HEAD_DIM), F32)],
        compiler_params=_params("arbitrary"),
    )(o, proj, gain, dy)


def _scan_matmul(x, mat):
    hi, mid, lo = _split3(x)
    return (_raw_dot("nn", None, hi, mat) + _raw_dot("nn", None, mid, mat)) + _raw_dot("nn", None, lo, mat)


def _attn_tile(q, k_ref, j, row0):
    ks = k_ref[pl.ds(pl.multiple_of(j * ATTN_TK, ATTN_TK), ATTN_TK), :]
    z = _raw_dot("nt", None, q, ks) * (HEAD_DIM ** -0.5)
    row = row0 + lax.broadcasted_iota(jnp.int32, z.shape, 0)
    col = j * ATTN_TK + lax.broadcasted_iota(jnp.int32, z.shape, 1)
    causal = col < row
    l0 = -_softplus(z)
    return ks, z, causal, l0, jnp.where(causal, l0, 0.0)


def _attn_specs(T, D):
    tq = min(ATTN_TQ, T)
    qspec = pl.BlockSpec((tq, HEAD_DIM), lambda h, i: (i, h))
    kspec = pl.BlockSpec((T, HEAD_DIM), lambda h, i: (0, h))
    return tq, qspec, kspec


def _sb_attn_fwd(q, k, v, name):
    T, D = q.shape
    tq, qspec, kspec = _attn_specs(T, D)
    per = tq // ATTN_TK

    def body(q_ref, k_ref, v_ref, o_ref):
        i = pl.program_id(1)
        qv = q_ref[...]
        jj = lax.broadcasted_iota(jnp.int32, (ATTN_TK, ATTN_TK), 0)
        ss = lax.broadcasted_iota(jnp.int32, (ATTN_TK, ATTN_TK), 1)
        later = (jj > ss).astype(BF16)

        def cond(c):
            return c[3]

        def step(c):
            j, r, acc, _ = c
            _, z, causal, l0, lf = _attn_tile(qv, k_ref, j, i * tq)
            after = _scan_matmul(lf, later)
            a = jnp.where(causal, jnp.exp(z + l0 + after + r), 0.0)
            vs = v_ref[pl.ds(pl.multiple_of(j * ATTN_TK, ATTN_TK), ATTN_TK), :]
            acc = acc + _raw_dot("nn", None, a.astype(BF16), vs)
            r = r + jnp.sum(lf, axis=1, keepdims=True)
            return j - 1, r, acc, (j > 0) & (jnp.max(r) >= -EXP_UNDERFLOW)

        init = ((i + 1) * per - 1, jnp.zeros((tq, 1), F32), jnp.zeros((tq, HEAD_DIM), F32), jnp.asarray(True))
        _, _, acc, _ = lax.while_loop(cond, step, init)
        o_ref[...] = acc.astype(BF16)

    return pl.pallas_call(
        body, name=name, grid=(D // HEAD_DIM, T // tq), in_specs=[qspec, kspec, kspec], out_specs=qspec,
        out_shape=jax.ShapeDtypeStruct((T, D), BF16), compiler_params=_params("parallel", "parallel"),
    )(q, k, v)


def _sb_attn_bwd(q, k, v, do, name):
    T, D = q.shape
    tq, qspec, kspec = _attn_specs(T, D)
    per = tq // ATTN_TK
    scale = HEAD_DIM ** -0.5

    def body(q_ref, do_ref, k_ref, v_ref, dq_ref, dk_ref, dv_ref):
        i = pl.program_id(1)

        @pl.when(i == 0)
        def _():
            dk_ref[...] = jnp.zeros_like(dk_ref)
            dv_ref[...] = jnp.zeros_like(dv_ref)

        qv = q_ref[...]
        dov = do_ref[...]
        jj = lax.broadcasted_iota(jnp.int32, (ATTN_TK, ATTN_TK), 0)
        ss = lax.broadcasted_iota(jnp.int32, (ATTN_TK, ATTN_TK), 1)
        later = (jj > ss).astype(BF16)
        earlier = (jj < ss).astype(BF16)
        nkb = (i + 1) * per

        def cond(c):
            return c[2]

        def find(c):
            j, r, _ = c
            lf = _attn_tile(qv, k_ref, j, i * tq)[4]
            r = r + jnp.sum(lf, axis=1, keepdims=True)
            return j - 1, r, (j > 0) & (jnp.max(r) >= -EXP_UNDERFLOW)

        j_end, total, _ = lax.while_loop(cond, find, (nkb - 1, jnp.zeros((tq, 1), F32), jnp.asarray(True)))

        def step(j, c):
            passed, p, dq = c
            ks, z, causal, l0, lf = _attn_tile(qv, k_ref, j, i * tq)
            rows = pl.ds(pl.multiple_of(j * ATTN_TK, ATTN_TK), ATTN_TK)
            block = jnp.sum(lf, axis=1, keepdims=True)
            after = _scan_matmul(lf, later) + (total - passed - block)
            a = jnp.where(causal, jnp.exp(z + l0 + after), 0.0)
            e = a * _raw_dot("nt", None, dov, v_ref[rows, :])
            before = _scan_matmul(e, earlier) + p
            sig = jnp.exp(z + l0)
            dz = (jnp.where(causal, e * (1.0 - sig) - sig * before, 0.0) * scale).astype(BF16)
            dq = dq + _raw_dot("nn", None, dz, ks)
            dk_ref[rows, :] += _raw_dot("tn", None, dz, qv)
            dv_ref[rows, :] += _raw_dot("tn", None, a.astype(BF16), dov)
            return passed + block, p + jnp.sum(e, axis=1, keepdims=True), dq

        zero = jnp.zeros((tq, 1), F32)
        dq_ref[...] = lax.fori_loop(j_end + 1, nkb, step, (zero, zero, jnp.zeros((tq, HEAD_DIM), F32)))[2]

    return pl.pallas_call(
        body, name=name, grid=(D // HEAD_DIM, T // tq), in_specs=[qspec, qspec, kspec, kspec],
        out_specs=[qspec, kspec, kspec], out_shape=[jax.ShapeDtypeStruct((T, D), F32)] * 3,
        compiler_params=_params("parallel", "arbitrary"),
    )(q, do, k, v)


def _exchange(send, gather, name):
    shape = send.shape if gather else send.shape[1:]

    def body(x_ref, out_ref, send_sems, recv_sems, local_sem):
        x, y, c = lax.axis_index("x"), lax.axis_index("y"), lax.axis_index("c")
        me = 4 * x + 2 * y + c

        def src(slot):
            return x_ref if gather else x_ref.at[slot]

        mine = pltpu.make_async_copy(src(me), out_ref.at[me], local_sem)
        mine.start()
        copies = []
        for kk in range(1, N_DEV):
            px = 1 - x if kk & 4 else x
            py = 1 - y if kk & 2 else y
            pc = 1 - c if kk & 1 else c
            peer = 4 * px + 2 * py + pc
            out_copy = pltpu.make_async_remote_copy(
                src_ref=src(peer), dst_ref=out_ref.at[me], send_sem=send_sems.at[kk - 1], recv_sem=recv_sems.at[kk - 1],
                device_id=(px, py, pc), device_id_type=MESH)
            in_copy = pltpu.make_async_remote_copy(
                src_ref=src(me), dst_ref=out_ref.at[peer], send_sem=send_sems.at[kk - 1], recv_sem=recv_sems.at[kk - 1],
                device_id=(px, py, pc), device_id_type=MESH)
            out_copy.start()
            copies.append((out_copy, in_copy))
        for out_copy, in_copy in copies:
            out_copy.wait_send()
            in_copy.wait_recv()
        mine.wait()

    any_spec = pl.BlockSpec(memory_space=pl.ANY)
    return pl.pallas_call(
        body, name=name, in_specs=[any_spec], out_specs=any_spec,
        out_shape=jax.ShapeDtypeStruct((N_DEV,) + tuple(shape), send.dtype),
        scratch_shapes=[pltpu.SemaphoreType.DMA((N_DEV - 1,)), pltpu.SemaphoreType.DMA((N_DEV - 1,)), pltpu.SemaphoreType.DMA],
    )(send)


def _adamw(parts, w, m, v, name):
    R, C = w.shape
    tr = _tile(R, 256)

    def body(p_ref, w_ref, m_ref, v_ref, g_ref, d_ref, nm_ref, nv_ref):
        g = p_ref[0]
        for j in range(1, N_DEV):
            g = g + p_ref[j]
        nm = ADAM_B1 * m_ref[...] + (1.0 - ADAM_B1) * g
        nv = ADAM_B2 * v_ref[...] + (1.0 - ADAM_B2) * (g * g)
        m_hat = nm / (1.0 - ADAM_B1 ** ADAM_STEP)
        v_hat = nv / (1.0 - ADAM_B2 ** ADAM_STEP)
        g_ref[...] = g
        d_ref[...] = -ADAM_LR * (m_hat / (jnp.sqrt(v_hat) + ADAM_EPS) + ADAM_WD * w_ref[...])
        nm_ref[...] = nm
        nv_ref[...] = nv

    row = pl.BlockSpec((tr, C), lambda i: (i, 0))
    return pl.pallas_call(
        body, name=name, grid=(R // tr,), in_specs=[pl.BlockSpec((N_DEV, tr, C), lambda i: (0, i, 0)), row, row, row],
        out_specs=[row] * 4, out_shape=[jax.ShapeDtypeStruct((R, C), F32)] * 4, compiler_params=_params("parallel"),
    )(parts, w, m, v)


def _pack_rows(n, align):
    rows = -(-n // PACK_COLS)
    return -(-rows // align) * align


def _pack(arrays, dtype, align, total_align=None, lead=0):
    pieces = []
    for a in arrays:
        head = a.shape[:lead]
        n = math.prod(a.shape[lead:])
        rows = _pack_rows(n, align)
        flat = jnp.pad(a.astype(dtype).reshape(head + (n,)), [(0, 0)] * lead + [(0, rows * PACK_COLS - n)])
        pieces.append(flat.reshape(head + (rows, PACK_COLS)))
    out = jnp.concatenate(pieces, axis=lead)
    if total_align:
        out = jnp.pad(out, [(0, 0)] * lead + [(0, -out.shape[lead] % total_align), (0, 0)])
    return out


def _unpack(buf, shapes, align):
    outs, r0 = [], 0
    lead = buf.shape[:-2]
    for shp in shapes:
        n = math.prod(shp)
        rows = _pack_rows(n, align)
        flat = buf[..., r0:r0 + rows, :].reshape(lead + (rows * PACK_COLS,))[..., :n]
        outs.append(flat.reshape(lead + tuple(shp)))
        r0 += rows
    return outs


def _join(blocks, axis):
    g = jnp.moveaxis(blocks, 0, axis)
    shp = list(g.shape)
    return g.reshape(shp[:axis] + [shp[axis] * shp[axis + 1]] + shp[axis + 2:])


def _split(full, axis):
    shp = list(full.shape)
    g = full.reshape(shp[:axis] + [N_DEV, shp[axis] // N_DEV] + shp[axis + 1:])
    return jnp.moveaxis(g, axis, 0)


def _pad_lanes(vec):
    return jnp.pad(vec.astype(F32), (0, LANES - vec.shape[0])).reshape(1, LANES)


def _ffn_fwd(x, norm, w_up, conv_w, w_down, tag):
    h = _rmsnorm_fwd(x, norm, f"{tag}_norm")
    u_pre = _matmul(h, w_up, "nn", F32, f"{tag}_up")
    act = _ffn_mid_fwd(u_pre, conv_w, f"{tag}_mid")
    out = _matmul(act, w_down, "nn", F32, f"{tag}_down", residual=x)
    return out, (x, h, u_pre, act)


def _ffn_bwd(saved, norm, w_up, conv_w, w_down, dx, dxb, tag):
    x, h, u_pre, act = saved
    d_w_down = _matmul(act, dxb, "tn", F32, f"{tag}_dwdown")
    dact = _matmul(dxb, w_down, "nt", F32, f"{tag}_dact")
    du = _ffn_mid_bwd(u_pre, conv_w, dact, f"{tag}_dmid")
    du_pre, d_conv = _conv_bwd(du, u_pre, conv_w, f"{tag}_dconv")
    d_w_up = _matmul(h, du_pre, "tn", F32, f"{tag}_dwup")
    dh = _matmul(du_pre, w_up, "nt", F32, f"{tag}_dh")
    dx, dxb, d_norm = _rmsnorm_bwd(x, norm, dh, dx, f"{tag}_dnorm")
    return dx, dxb, dict(norm=d_norm, w_up=d_w_up, conv=d_conv, w_down=d_w_down)


def _gdn_fwd(x, norm, w_in, conv_w, alog, dtb, gain, w_out, H, tag):
    h = _rmsnorm_fwd(x, norm, f"{tag}_norm")
    proj = _matmul(h, w_in, "nn", F32, f"{tag}_in")
    q, k, v, gates = _gdn_pre_fwd(proj, conv_w, alog, dtb, H, f"{tag}_pre")
    o, s_hist = _gdn_rule_fwd(q, k, v, gates, H, f"{tag}_rule")
    y = _gdn_post_fwd(o, proj, gain, f"{tag}_post")
    out = _matmul(y, w_out, "nn", F32, f"{tag}_out", residual=x)
    return out, (x, h, proj, q, k, v, gates, o, s_hist, y)


def _gdn_bwd(saved, norm, w_in, conv_w, alog, dtb, gain, w_out, dx, dxb, H, tag):
    x, h, proj, q, k, v, gates, o, s_hist, y = saved
    D = H * HEAD_DIM
    d_w_out = _matmul(y, dxb, "tn", F32, f"{tag}_dwout")
    dy = _matmul(dxb, w_out, "nt", F32, f"{tag}_dy")
    do, dz, d_gain = _gdn_post_bwd(o, proj, gain, dy, f"{tag}_dpost")
    dq, dk, dv, dgates = _gdn_rule_bwd(q, k, v, gates, s_hist, do, H, f"{tag}_drule")
    dc, dab, d_alog, d_dtb = _gdn_pre_bwd(proj, conv_w, alog, dtb, dq, dk, dv, dgates, H, f"{tag}_dpre")
    dqkv, d_conv = _conv_bwd(dc, proj, conv_w, f"{tag}_dconv")
    dproj = jnp.concatenate([dqkv, dz, dab], axis=1)
    d_w_in = _matmul(h, dproj, "tn", F32, f"{tag}_dwin")
    dh = _matmul(dproj, w_in, "nt", F32, f"{tag}_dh")
    dx, dxb, d_norm = _rmsnorm_bwd(x, norm, dh, dx, f"{tag}_dnorm")
    grads = dict(norm=d_norm, w_in=d_w_in[:, :4 * D + 2 * H], conv=d_conv, alog=d_alog[0, :H], dtb=d_dtb[0, :H],
                 gain=d_gain[0], w_out=d_w_out)
    return dx, dxb, grads


def _sb_fwd(x, norm, w_q, q_gain, w_out, k, v, tag):
    h = _rmsnorm_fwd(x, norm, f"{tag}_norm")
    qp = _matmul(h, w_q, "nn", F32, f"{tag}_q")
    q = _qnorm_fwd(qp, q_gain, f"{tag}_qnorm")
    o = _sb_attn_fwd(q, k, v, f"{tag}_attn")
    out = _matmul(o, w_out, "nn", F32, f"{tag}_out", residual=x)
    return out, (x, h, qp, q, o)


def _sb_bwd(saved, norm, w_q, q_gain, w_out, k, v, dx, dxb, tag):
    x, h, qp, q, o = saved
    d_w_out = _matmul(o, dxb, "tn", F32, f"{tag}_dwout")
    do = _matmul(dxb, w_out, "nt", BF16, f"{tag}_do")
    dq, dk, dv = _sb_attn_bwd(q, k, v, do, f"{tag}_dattn")
    dqp, d_qgain = _qnorm_bwd(qp, q_gain, dq, f"{tag}_dqnorm")
    d_w_q = _matmul(h, dqp, "tn", F32, f"{tag}_dwq")
    dh = _matmul(dqp, w_q, "nt", F32, f"{tag}_dh")
    dx, dxb, d_norm = _rmsnorm_bwd(x, norm, dh, dx, f"{tag}_dnorm")
    return dx, dxb, dk, dv, dict(norm=d_norm, w_q=d_w_q, q_gain=d_qgain[0], w_out=d_w_out)


def _local_step(x, target, W):
    T, D = x.shape
    H = D // HEAD_DIM
    n_a, n_b = W["a_w_in"].shape[0], W["b_w_q"].shape[0]
    row = lambda a: a.reshape(1, -1)
    saved = []
    for l in range(n_a):
        x, s = _gdn_fwd(x, row(W["a_norm"][l]), W["a_w_in"][l], W["a_conv"][l], _pad_lanes(W["a_log"][l]),
                        _pad_lanes(W["a_dt_bias"][l]), row(W["a_out_norm"][l]), W["a_w_out"][l], H, f"gdn{l}")
        saved.append(s)
        x, s = _ffn_fwd(x, row(W["ffn_norm"][l]), W["ffn_w_up"][l], W["ffn_conv"][l], W["ffn_w_down"][l], f"ffn{l}")
        saved.append(s)
    x_kv = x
    h_kv = _rmsnorm_fwd(x_kv, row(W["kv_norm"]), "kv_norm")
    kv = _matmul(h_kv, W["w_kv"], "nn", F32, "kv_proj")
    k, v = _kv_post_fwd(kv, row(W["k_norm"]), "kv_post")
    for j in range(n_b):
        l = n_a + j
        x, s = _sb_fwd(x, row(W["b_norm"][j]), W["b_w_q"][j], row(W["q_norm"][j]), W["b_w_out"][j], k, v, f"sb{j}")
        saved.append(s)
        x, s = _ffn_fwd(x, row(W["ffn_norm"][l]), W["ffn_w_up"][l], W["ffn_conv"][l], W["ffn_w_down"][l], f"ffn{l}")
        saved.append(s)

    loss, dx, dxb = _loss_grad(x, target, "loss")

    G = {name: [None] * W[name].shape[0] for name in
         ("a_norm", "a_w_in", "a_conv", "a_log", "a_dt_bias", "a_out_norm", "a_w_out", "b_norm", "b_w_q", "q_norm", "b_w_out",
          "ffn_norm", "ffn_w_up", "ffn_conv", "ffn_w_down")}

    def take_ffn(l, g):
        G["ffn_norm"][l], G["ffn_w_up"][l], G["ffn_conv"][l], G["ffn_w_down"][l] = g["norm"][0], g["w_up"], g["conv"], g["w_down"]

    dks, dvs = [], []
    for j in reversed(range(n_b)):
        l = n_a + j
        dx, dxb, g = _ffn_bwd(saved.pop(), row(W["ffn_norm"][l]), W["ffn_w_up"][l], W["ffn_conv"][l], W["ffn_w_down"][l],
                              dx, dxb, f"ffn{l}")
        take_ffn(l, g)
        dx, dxb, dk, dv, g = _sb_bwd(saved.pop(), row(W["b_norm"][j]), W["b_w_q"][j], row(W["q_norm"][j]), W["b_w_out"][j],
                                     k, v, dx, dxb, f"sb{j}")
        G["b_norm"][j], G["b_w_q"][j], G["q_norm"][j], G["b_w_out"][j] = g["norm"][0], g["w_q"], g["q_gain"], g["w_out"]
        dks.append(dk)
        dvs.append(dv)
    assert n_b == 2
    dkv, d_k_norm = _kv_post_bwd(kv, row(W["k_norm"]), dks[0], dks[1], dvs[0], dvs[1], "kv_dpost")
    G["w_kv"] = _matmul(h_kv, dkv, "tn", F32, "kv_dw")
    dh_kv = _matmul(dkv, W["w_kv"], "nt", F32, "kv_dh")
    dx, dxb, d_kv_norm = _rmsnorm_bwd(x_kv, row(W["kv_norm"]), dh_kv, dx, "kv_dnorm")
    G["k_norm"], G["kv_norm"] = d_k_norm[0], d_kv_norm[0]
    for l in reversed(range(n_a)):
        dx, dxb, g = _ffn_bwd(saved.pop(), row(W["ffn_norm"][l]), W["ffn_w_up"][l], W["ffn_conv"][l], W["ffn_w_down"][l],
                              dx, dxb, f"ffn{l}")
        take_ffn(l, g)
        dx, dxb, g = _gdn_bwd(saved.pop(), row(W["a_norm"][l]), W["a_w_in"][l], W["a_conv"][l], _pad_lanes(W["a_log"][l]),
                              _pad_lanes(W["a_dt_bias"][l]), row(W["a_out_norm"][l]), W["a_w_out"][l], dx, dxb, H, f"gdn{l}")
        (G["a_norm"][l], G["a_w_in"][l], G["a_conv"][l], G["a_log"][l], G["a_dt_bias"][l], G["a_out_norm"][l],
         G["a_w_out"][l]) = g["norm"][0], g["w_in"], g["conv"], g["alog"], g["dtb"], g["gain"], g["w_out"]
    grads = {name: (jnp.stack(val) if isinstance(val, list) else val) for name, val in G.items()}
    return loss, dx, grads


WEIGHTS = ["a_norm", "a_w_in", "a_conv", "a_log", "a_dt_bias", "a_out_norm", "a_w_out", "kv_norm", "w_kv", "k_norm", "b_norm",
           "b_w_q", "q_norm", "b_w_out", "ffn_norm", "ffn_w_up", "ffn_conv", "ffn_w_down"]
SHARD_AXIS = {"a_norm": 1, "a_w_in": 2, "a_conv": 2, "a_w_out": 1, "w_kv": 1, "b_w_q": 1, "b_w_out": 1, "ffn_w_up": 2,
              "ffn_conv": 2, "ffn_w_down": 1}
MATMUL_WEIGHTS = ["a_w_in", "a_w_out", "w_kv", "b_w_q", "b_w_out", "ffn_w_up", "ffn_w_down"]
VECTOR_WEIGHTS = ["a_norm", "a_conv", "ffn_conv"]
SHARDED = MATMUL_WEIGHTS + VECTOR_WEIGHTS
REPLICATED = [n for n in WEIGHTS if n not in SHARD_AXIS]
ADAM_ROW_TILE = 256


def kernel(x, a_norm, a_w_in, a_conv, a_log, a_dt_bias, a_out_norm, a_w_out, kv_norm, w_kv, k_norm, b_norm, b_w_q, q_norm, b_w_out, ffn_norm, ffn_w_up, ffn_conv, ffn_w_down, loss_target, m_a_norm, m_a_w_in, m_a_conv, m_a_log, m_a_dt_bias, m_a_out_norm, m_a_w_out, m_kv_norm, m_w_kv, m_k_norm, m_b_norm, m_b_w_q, m_q_norm, m_b_w_out, m_ffn_norm, m_ffn_w_up, m_ffn_conv, m_ffn_w_down, v_a_norm, v_a_w_in, v_a_conv, v_a_log, v_a_dt_bias, v_a_out_norm, v_a_w_out, v_kv_norm, v_w_kv, v_k_norm, v_b_norm, v_b_w_q, v_q_norm, v_b_w_out, v_ffn_norm, v_ffn_w_up, v_ffn_conv, v_ffn_w_down):
    w = dict(a_norm=a_norm, a_w_in=a_w_in, a_conv=a_conv, a_log=a_log, a_dt_bias=a_dt_bias, a_out_norm=a_out_norm, a_w_out=a_w_out,
             kv_norm=kv_norm, w_kv=w_kv, k_norm=k_norm, b_norm=b_norm, b_w_q=b_w_q, q_norm=q_norm, b_w_out=b_w_out,
             ffn_norm=ffn_norm, ffn_w_up=ffn_w_up, ffn_conv=ffn_conv, ffn_w_down=ffn_w_down)
    m = dict(a_norm=m_a_norm, a_w_in=m_a_w_in, a_conv=m_a_conv, a_log=m_a_log, a_dt_bias=m_a_dt_bias, a_out_norm=m_a_out_norm,
             a_w_out=m_a_w_out, kv_norm=m_kv_norm, w_kv=m_w_kv, k_norm=m_k_norm, b_norm=m_b_norm, b_w_q=m_b_w_q, q_norm=m_q_norm,
             b_w_out=m_b_w_out, ffn_norm=m_ffn_norm, ffn_w_up=m_ffn_w_up, ffn_conv=m_ffn_conv, ffn_w_down=m_ffn_w_down)
    v = dict(a_norm=v_a_norm, a_w_in=v_a_w_in, a_conv=v_a_conv, a_log=v_a_log, a_dt_bias=v_a_dt_bias, a_out_norm=v_a_out_norm,
             a_w_out=v_a_w_out, kv_norm=v_kv_norm, w_kv=v_w_kv, k_norm=v_k_norm, b_norm=v_b_norm, b_w_q=v_b_w_q, q_norm=v_q_norm,
             b_w_out=v_b_w_out, ffn_norm=v_ffn_norm, ffn_w_up=v_ffn_w_up, ffn_conv=v_ffn_conv, ffn_w_down=v_ffn_w_down)
    D = x.shape[-1]
    H = D // HEAD_DIM

    mat_shapes = [w[n].shape for n in MATMUL_WEIGHTS]
    vec_shapes = [w[n].shape for n in VECTOR_WEIGHTS]
    mats = _exchange(_pack([w[n] for n in MATMUL_WEIGHTS], BF16, PACK_ROW_ALIGN), True, "gather_matmul_weights")
    vecs = _exchange(_pack([w[n] for n in VECTOR_WEIGHTS], F32, SUBLANES), True, "gather_vector_weights")
    W = {n: w[n] for n in REPLICATED}
    for n, blocks in zip(MATMUL_WEIGHTS, _unpack(mats, mat_shapes, PACK_ROW_ALIGN)):
        W[n] = _join(blocks, SHARD_AXIS[n])
    for n, blocks in zip(VECTOR_WEIGHTS, _unpack(vecs, vec_shapes, SUBLANES)):
        W[n] = _join(blocks, SHARD_AXIS[n])
    W["a_w_in"] = jnp.pad(W["a_w_in"], ((0, 0), (0, 0), (0, 4 * D + LANES - W["a_w_in"].shape[2])))

    loss, grad_x, G = _local_step(x[0], loss_target[0], W)

    send = _pack([_split(G[n], SHARD_AXIS[n]) for n in SHARDED], F32, SUBLANES, ADAM_ROW_TILE, lead=1)
    parts = _exchange(send, False, "scatter_gradients")
    shard_shapes = [w[n].shape for n in SHARDED]
    pk = lambda d: _pack([d[n] for n in SHARDED], F32, SUBLANES, ADAM_ROW_TILE)
    outs = _adamw(parts, pk(w), pk(m), pk(v), "adamw_shards")
    big = [dict(zip(SHARDED, _unpack(o, shard_shapes, SUBLANES))) for o in outs]

    rep_shapes = [w[n].shape for n in REPLICATED]
    rp = lambda d: _pack([d[n] for n in REPLICATED], F32, SUBLANES, SUBLANES)
    rep_parts = _exchange(rp(G), True, "gather_replicated_gradients")
    outs = _adamw(rep_parts, rp(w), rp(m), rp(v), "adamw_replicated")
    small = [dict(zip(REPLICATED, _unpack(o, rep_shapes, SUBLANES))) for o in outs]

    total_loss = lax.psum(loss[0, 0], ("x", "y", "c"))
    result = [total_loss, grad_x[None]]
    for kind in range(4):
        result += [big[kind][n] if n in SHARD_AXIS else small[kind][n] for n in WEIGHTS]
    return tuple(result)
```

```python
import functools
import math

import jax
import jax.numpy as jnp
from jax import lax
from jax.experimental import pallas as pl
from jax.experimental.pallas import tpu as pltpu

F32 = jnp.float32
BF16 = jnp.bfloat16
X3 = "x3"
LHS01 = "lhs01"
MESH = pl.DeviceIdType.MESH

N_DEV = 8
LANES = 128
SUBLANES = 8
PACK_COLS = 1024
PACK_ROW_ALIGN = 16
HEAD_DIM = 128
GDN_CHUNK = 128
EPS = 1e-6
ATTN_TQ = 256
ATTN_TK = 128
EXP_UNDERFLOW = 104.0
VMEM_LIMIT = 48 * 1024 * 1024
MATMUL_TILE = 1408

ADAM_LR = 0.001
ADAM_B1 = 0.9
ADAM_B2 = 0.999
ADAM_EPS = 1e-08
ADAM_WD = 0.01
ADAM_STEP = 10


def _tile(n, pref):
    if n <= pref:
        return n
    best = 0
    for t in range(LANES, pref + 1, LANES):
        if n % t == 0:
            best = t
    assert best, (n, pref)
    return best


def _params(*sem):
    return pltpu.CompilerParams(dimension_semantics=tuple(sem), vmem_limit_bytes=VMEM_LIMIT)


_DIMS = {"nn": (((1,), (0,)), ((), ())), "nt": (((1,), (1,)), ((), ())), "tn": (((0,), (0,)), ((), ()))}


def _split3(x):
    hi = x.astype(BF16)
    r1 = x - hi.astype(F32)
    mid = r1.astype(BF16)
    lo = (r1 - mid.astype(F32)).astype(BF16)
    return hi, mid, lo


def _raw_dot(mode, prec, a, b):
    if prec is None:
        return lax.dot_general(a, b, _DIMS[mode], preferred_element_type=F32)
    if prec == LHS01:
        b_hi, b_mid, b_lo = _split3(b)
        a_hi = a.astype(BF16)
        terms = [(a_hi, b_lo), (a_hi, b_mid), (a_hi, b_hi)]
    else:
        (a_hi, a_mid, _), (b_hi, b_mid, _) = _split3(a), _split3(b)
        terms = [(a_mid, b_hi), (a_hi, b_mid), (a_hi, b_hi)]
    out = None
    for ta, tb in terms:
        term = lax.dot_general(ta, tb, _DIMS[mode], preferred_element_type=F32)
        out = term if out is None else out + term
    return out


@functools.partial(jax.custom_vjp, nondiff_argnums=(0, 1))
def _dot(mode, prec, a, b):
    return _raw_dot(mode, prec, a, b)


def _dot_fwd(mode, prec, a, b):
    return _raw_dot(mode, prec, a, b), (a, b)


def _dot_bwd(mode, prec, res, ct):
    a, b = res
    if prec == LHS01:
        assert mode == "nn"
        return jnp.zeros_like(a), _dot("tn", LHS01, a, ct)
    if mode == "nn":
        return _dot("nt", prec, ct, b), _dot("tn", prec, a, ct)
    if mode == "nt":
        return _dot("nn", prec, ct, b), _dot("tn", prec, ct, a)
    return _dot("nt", prec, b, ct), _dot("nn", prec, a, ct)


_dot.defvjp(_dot_fwd, _dot_bwd)


def _softplus(x):
    return jnp.maximum(x, 0.0) + jnp.log(1.0 + jnp.exp(-jnp.abs(x)))


def _matmul(a, b, mode, out_dtype, name, residual=None):
    if mode == "nn":
        (M, K), N = a.shape, b.shape[1]
    elif mode == "nt":
        (M, K), N = a.shape, b.shape[0]
    else:
        (K, M), N = a.shape, b.shape[1]
    tm, tn, tk = _tile(M, MATMUL_TILE), _tile(N, MATMUL_TILE), _tile(K, MATMUL_TILE)
    nk = K // tk

    def body(*refs):
        if residual is None:
            a_ref, b_ref, o_ref, acc_ref = refs
        else:
            a_ref, b_ref, r_ref, o_ref, acc_ref = refs

        def finish(r):
            if residual is not None:
                r = r + r_ref[...]
            o_ref[...] = r.astype(out_dtype)

        if nk == 1:
            finish(_raw_dot(mode, None, a_ref[...], b_ref[...]))
            return
        k = pl.program_id(2)

        @pl.when(k == 0)
        def _():
            acc_ref[...] = jnp.zeros_like(acc_ref)

        acc_ref[...] += _raw_dot(mode, None, a_ref[...], b_ref[...])

        @pl.when(k == nk - 1)
        def _():
            finish(acc_ref[...])

    a_spec = pl.BlockSpec((tk, tm), lambda i, j, k: (k, i)) if mode == "tn" else pl.BlockSpec((tm, tk), lambda i, j, k: (i, k))
    b_spec = pl.BlockSpec((tn, tk), lambda i, j, k: (j, k)) if mode == "nt" else pl.BlockSpec((tk, tn), lambda i, j, k: (k, j))
    o_spec = pl.BlockSpec((tm, tn), lambda i, j, k: (i, j))
    in_specs, args = [a_spec, b_spec], [a, b]
    if residual is not None:
        in_specs.append(o_spec)
        args.append(residual)
    return pl.pallas_call(
        body, name=name, grid=(M // tm, N // tn, nk), in_specs=in_specs, out_specs=o_spec,
        out_shape=jax.ShapeDtypeStruct((M, N), out_dtype),
        scratch_shapes=[pltpu.VMEM((tm, tn) if nk > 1 else (SUBLANES, LANES), F32)],
        compiler_params=_params("parallel", "parallel", "arbitrary"),
    )(*args)


def _rmsnorm_fwd(x, g, name):
    T, D = x.shape
    tt = _tile(T, 512)

    def body(x_ref, g_ref, o_ref):
        xv = x_ref[...]
        r = lax.rsqrt(jnp.mean(xv * xv, axis=-1, keepdims=True) + EPS)
        o_ref[...] = (xv * r * g_ref[...]).astype(BF16)

    row = pl.BlockSpec((tt, D), lambda i: (i, 0))
    return pl.pallas_call(
        body, name=name, grid=(T // tt,), in_specs=[row, pl.BlockSpec((1, D), lambda i: (0, 0))], out_specs=row,
        out_shape=jax.ShapeDtypeStruct((T, D), BF16), compiler_params=_params("parallel"),
    )(x, g)


def _rmsnorm_bwd(x, g, dh, dx_in, name):
    T, D = x.shape
    tt = _tile(T, 512)

    def body(x_ref, g_ref, dh_ref, dxin_ref, dx_ref, dxb_ref, dg_ref):
        xv = x_ref[...]
        dhv = dh_ref[...].astype(F32)
        r = lax.rsqrt(jnp.mean(xv * xv, axis=-1, keepdims=True) + EPS)
        dn = dhv * g_ref[...]
        dx = r * dn - xv * (r * r * r) * jnp.mean(dn * xv, axis=-1, keepdims=True) + dxin_ref[...]
        dx_ref[...] = dx
        dxb_ref[...] = dx.astype(BF16)

        @pl.when(pl.program_id(0) == 0)
        def _():
            dg_ref[...] = jnp.zeros_like(dg_ref)

        dg_ref[...] += jnp.sum(dhv * xv * r, axis=0, keepdims=True)

    row = pl.BlockSpec((tt, D), lambda i: (i, 0))
    vec = pl.BlockSpec((1, D), lambda i: (0, 0))
    return pl.pallas_call(
        body, name=name, grid=(T // tt,), in_specs=[row, vec, row, row], out_specs=[row, row, vec],
        out_shape=[jax.ShapeDtypeStruct((T, D), F32), jax.ShapeDtypeStruct((T, D), BF16), jax.ShapeDtypeStruct((1, D), F32)],
        compiler_params=_params("arbitrary"),
    )(x, g, dh, dx_in)


def _headnorm(x, gain):
    outs = []
    for h in range(x.shape[1] // HEAD_DIM):
        xh = x[:, h * HEAD_DIM:(h + 1) * HEAD_DIM]
        outs.append(xh * lax.rsqrt(jnp.mean(xh * xh, axis=-1, keepdims=True) + EPS) * gain)
    return jnp.concatenate(outs, axis=1)


def _qnorm_fwd(qp, gain, name):
    T, D = qp.shape
    tt = _tile(T, 512)

    def body(x_ref, g_ref, o_ref):
        o_ref[...] = _headnorm(x_ref[...], g_ref[...]).astype(BF16)

    row = pl.BlockSpec((tt, D), lambda i: (i, 0))
    return pl.pallas_call(
        body, name=name, grid=(T // tt,), in_specs=[row, pl.BlockSpec((1, HEAD_DIM), lambda i: (0, 0))], out_specs=row,
        out_shape=jax.ShapeDtypeStruct((T, D), BF16), compiler_params=_params("parallel"),
    )(qp, gain)


def _qnorm_bwd(qp, gain, dq, name):
    T, D = qp.shape
    tt = _tile(T, 512)

    def body(x_ref, g_ref, dq_ref, dx_ref, dg_ref):
        _, vjp = jax.vjp(_headnorm, x_ref[...], g_ref[...])
        dx, dg = vjp(dq_ref[...])
        dx_ref[...] = dx.astype(BF16)

        @pl.when(pl.program_id(0) == 0)
        def _():
            dg_ref[...] = jnp.zeros_like(dg_ref)

        dg_ref[...] += dg

    row = pl.BlockSpec((tt, D), lambda i: (i, 0))
    vec = pl.BlockSpec((1, HEAD_DIM), lambda i: (0, 0))
    return pl.pallas_call(
        body, name=name, grid=(T // tt,), in_specs=[row, vec, row], out_specs=[row, vec],
        out_shape=[jax.ShapeDtypeStruct((T, D), BF16), jax.ShapeDtypeStruct((1, HEAD_DIM), F32)],
        compiler_params=_params("arbitrary"),
    )(qp, gain, dq)


def _kv_post_fwd(kv, gain, name):
    T, D2 = kv.shape
    D = D2 // 2
    tt = _tile(T, 512)

    def body(k_ref, v_ref, g_ref, ko_ref, vo_ref):
        ko_ref[...] = _headnorm(k_ref[...], g_ref[...]).astype(BF16)
        vo_ref[...] = v_ref[...].astype(BF16)

    row = pl.BlockSpec((tt, D), lambda i: (i, 0))
    return pl.pallas_call(
        body, name=name, grid=(T // tt,),
        in_specs=[row, pl.BlockSpec((tt, D), lambda i: (i, 1)), pl.BlockSpec((1, HEAD_DIM), lambda i: (0, 0))],
        out_specs=[row, row], out_shape=[jax.ShapeDtypeStruct((T, D), BF16)] * 2, compiler_params=_params("parallel"),
    )(kv, kv, gain)


def _kv_post_bwd(kv, gain, dk_a, dk_b, dv_a, dv_b, name):
    T, D2 = kv.shape
    D = D2 // 2
    tt = _tile(T, 512)

    def body(k_ref, g_ref, dka_ref, dkb_ref, dva_ref, dvb_ref, dkv_ref, dg_ref):
        _, vjp = jax.vjp(_headnorm, k_ref[...], g_ref[...])
        dx, dg = vjp(dka_ref[...] + dkb_ref[...])
        dkv_ref[:, :D] = dx.astype(BF16)
        dkv_ref[:, D:] = (dva_ref[...] + dvb_ref[...]).astype(BF16)

        @pl.when(pl.program_id(0) == 0)
        def _():
            dg_ref[...] = jnp.zeros_like(dg_ref)

        dg_ref[...] += dg

    row = pl.BlockSpec((tt, D), lambda i: (i, 0))
    vec = pl.BlockSpec((1, HEAD_DIM), lambda i: (0, 0))
    return pl.pallas_call(
        body, name=name, grid=(T // tt,), in_specs=[row, vec, row, row, row, row],
        out_specs=[pl.BlockSpec((tt, D2), lambda i: (i, 0)), vec],
        out_shape=[jax.ShapeDtypeStruct((T, D2), BF16), jax.ShapeDtypeStruct((1, HEAD_DIM), F32)],
        compiler_params=_params("arbitrary"),
    )(kv, gain, dk_a, dk_b, dv_a, dv_b)


def _loss_grad(y, target, name):
    T, D = y.shape
    tt = _tile(T, 512)
    nt = T // tt

    def body(y_ref, t_ref, loss_ref, dy_ref, dyb_ref, acc_ref):
        i = pl.program_id(0)
        d = y_ref[...] - t_ref[...]
        dy = d * (1.0 / D)
        dy_ref[...] = dy
        dyb_ref[...] = dy.astype(BF16)

        @pl.when(i == 0)
        def _():
            acc_ref[...] = jnp.zeros_like(acc_ref)

        acc_ref[...] += jnp.sum(d * d, axis=0, keepdims=True)

        @pl.when(i == nt - 1)
        def _():
            loss_ref[...] = jnp.zeros_like(loss_ref) + jnp.sum(acc_ref[...]) * (0.5 / D)

    row = pl.BlockSpec((tt, D), lambda i: (i, 0))
    return pl.pallas_call(
        body, name=name, grid=(nt,), in_specs=[row, row], out_specs=[pl.BlockSpec((1, LANES), lambda i: (0, 0)), row, row],
        out_shape=[jax.ShapeDtypeStruct((1, LANES), F32), jax.ShapeDtypeStruct((T, D), F32), jax.ShapeDtypeStruct((T, D), BF16)],
        scratch_shapes=[pltpu.VMEM((1, D), F32)], compiler_params=_params("arbitrary"),
    )(y, target)


def _conv_tile(halo, main, w):
    K = w.shape[0]
    xc = jnp.concatenate([halo, main], axis=0)
    y = main * w[K - 1:K, :]
    for k in range(K - 1):
        y = y + pltpu.roll(xc, K - 1 - k, 0)[SUBLANES:, :] * w[k:k + 1, :]
    return y


def _prev_halo(tt, col):
    return lambda cb: pl.BlockSpec((SUBLANES, cb), lambda j, i: (jnp.maximum(i * (tt // SUBLANES) - 1, 0), col(j)))


def _conv_bwd(dy, x, w, name):
    T, C = dy.shape
    K = w.shape[0]
    tt, cb = _tile(T, 256), _tile(C, 1024)
    nt = T // tt

    def body(dy_ref, dyn_ref, x_ref, xp_ref, w_ref, dx_ref, dw_ref):
        i = pl.program_id(1)
        wv = w_ref[...]
        dyv = dy_ref[...]
        nxt = jnp.where(i == nt - 1, 0.0, dyn_ref[...])
        prev = jnp.where(i == 0, 0.0, xp_ref[...])
        dyc = jnp.concatenate([dyv, nxt], axis=0)
        xc = jnp.concatenate([prev, x_ref[...]], axis=0)
        tap = lax.broadcasted_iota(jnp.int32, (K, cb), 0)
        dx = dyv * wv[K - 1:K, :]
        dw = jnp.where(tap == K - 1, jnp.sum(dyv * x_ref[...], axis=0, keepdims=True), 0.0)
        for k in range(K - 1):
            s = K - 1 - k
            dx = dx + pltpu.roll(dyc, SUBLANES - s, 0)[SUBLANES:, :] * wv[k:k + 1, :]
            dw = dw + jnp.where(tap == k, jnp.sum(dyv * pltpu.roll(xc, s, 0)[SUBLANES:, :], axis=0, keepdims=True), 0.0)
        dx_ref[...] = dx.astype(BF16)

        @pl.when(i == 0)
        def _():
            dw_ref[...] = jnp.zeros_like(dw_ref)

        dw_ref[...] += dw

    main = pl.BlockSpec((tt, cb), lambda j, i: (i, j))
    nxt = pl.BlockSpec((SUBLANES, cb), lambda j, i: (jnp.minimum((i + 1) * (tt // SUBLANES), T // SUBLANES - 1), j))
    wspec = pl.BlockSpec((K, cb), lambda j, i: (0, j))
    return pl.pallas_call(
        body, name=name, grid=(C // cb, nt), in_specs=[main, nxt, main, _prev_halo(tt, lambda j: j)(cb), wspec],
        out_specs=[main, wspec], out_shape=[jax.ShapeDtypeStruct((T, C), BF16), jax.ShapeDtypeStruct((K, C), F32)],
        compiler_params=_params("parallel", "arbitrary"),
    )(dy, dy, x, x, w)


def _ffn_mid_specs(T, F):
    tt, cb = _tile(T, 256), _tile(F, 512)
    nf = F // cb
    gate = pl.BlockSpec((tt, cb), lambda j, i: (i, j % nf))
    up = pl.BlockSpec((tt, cb), lambda j, i: (i, nf + j % nf))
    gate_h = _prev_halo(tt, lambda j: j % nf)(cb)
    up_h = _prev_halo(tt, lambda j: nf + j % nf)(cb)
    wg = pl.BlockSpec((3, cb), lambda j, i: (0, j % nf))
    wu = pl.BlockSpec((3, cb), lambda j, i: (0, nf + j % nf))
    return tt, cb, nf, [gate, gate_h, up, up_h, wg, wu]


def _ffn_mid_fwd(u_pre, w, name):
    T, F2 = u_pre.shape
    F = F2 // 2
    tt, cb, nf, specs = _ffn_mid_specs(T, F)

    def body(g_ref, gh_ref, u_ref, uh_ref, wg_ref, wu_ref, o_ref):
        first = pl.program_id(1) == 0
        ug = _conv_tile(jnp.where(first, 0.0, gh_ref[...]), g_ref[...], wg_ref[...])
        uu = _conv_tile(jnp.where(first, 0.0, uh_ref[...]), u_ref[...], wu_ref[...])
        o_ref[...] = (ug * jax.nn.sigmoid(ug) * uu).astype(BF16)

    return pl.pallas_call(
        body, name=name, grid=(nf, T // tt), in_specs=specs, out_specs=pl.BlockSpec((tt, cb), lambda j, i: (i, j)),
        out_shape=jax.ShapeDtypeStruct((T, F), BF16), compiler_params=_params("parallel", "parallel"),
    )(u_pre, u_pre, u_pre, u_pre, w, w)


def _ffn_mid_bwd(u_pre, w, dact, name):
    T, F2 = u_pre.shape
    F = F2 // 2
    K = w.shape[0]
    tt = _tile(T, 128)

    def body(x_ref, xh_ref, w_ref, da_ref, o_ref):
        c = _conv_tile(jnp.where(pl.program_id(0) == 0, 0.0, xh_ref[...]), x_ref[...], w_ref[...])
        ug, uu = c[:, :F], c[:, F:]
        sg = jax.nn.sigmoid(ug)
        da = da_ref[...]
        o_ref[:, :F] = da * uu * sg * (1.0 + ug * (1.0 - sg))
        o_ref[:, F:] = da * ug * sg

    main = pl.BlockSpec((tt, F2), lambda i: (i, 0))
    halo = pl.BlockSpec((SUBLANES, F2), lambda i: (jnp.maximum(i * (tt // SUBLANES) - 1, 0), 0))
    return pl.pallas_call(
        body, name=name, grid=(T // tt,),
        in_specs=[main, halo, pl.BlockSpec((K, F2), lambda i: (0, 0)), pl.BlockSpec((tt, F), lambda i: (i, 0))],
        out_specs=main, out_shape=jax.ShapeDtypeStruct((T, F2), F32), compiler_params=_params("parallel"),
    )(u_pre, u_pre, w, dact)


def _gdn_act(c, ab, alog, dtb, H):
    D = H * HEAD_DIM
    s = c * jax.nn.sigmoid(c)
    qs, ks = [], []
    for h in range(H):
        qh = s[:, h * HEAD_DIM:(h + 1) * HEAD_DIM]
        kh = s[:, D + h * HEAD_DIM:D + (h + 1) * HEAD_DIM]
        qs.append(qh * lax.rsqrt(jnp.sum(qh * qh, axis=-1, keepdims=True) + EPS) * (HEAD_DIM ** -0.5))
        ks.append(kh * lax.rsqrt(jnp.sum(kh * kh, axis=-1, keepdims=True) + EPS))
    g = -jnp.exp(alog) * _softplus(ab + dtb)
    lane = lax.broadcasted_iota(jnp.int32, ab.shape, 1)
    gates = jnp.where(lane < H, g, jax.nn.sigmoid(ab))
    return jnp.concatenate(qs, axis=1), jnp.concatenate(ks, axis=1), s[:, 2 * D:], gates


def _gdn_pre_specs(T, D):
    tt = _tile(T, 256)
    main = pl.BlockSpec((tt, 3 * D), lambda i: (i, 0))
    halo = pl.BlockSpec((SUBLANES, 3 * D), lambda i: (jnp.maximum(i * (tt // SUBLANES) - 1, 0), 0))
    ab = pl.BlockSpec((tt, LANES), lambda i: (i, 4 * D // LANES))
    return tt, main, halo, ab


def _gdn_pre_fwd(proj, conv_w, alog, dtb, H, name):
    T = proj.shape[0]
    D = H * HEAD_DIM
    tt, main, halo, ab = _gdn_pre_specs(T, D)
    K = conv_w.shape[0]

    def body(x_ref, xh_ref, ab_ref, w_ref, al_ref, dt_ref, q_ref, k_ref, v_ref, g_ref):
        c = _conv_tile(jnp.where(pl.program_id(0) == 0, 0.0, xh_ref[...]), x_ref[...], w_ref[...])
        q, k, v, gates = _gdn_act(c, ab_ref[...], al_ref[...], dt_ref[...], H)
        q_ref[...] = q
        k_ref[...] = k
        v_ref[...] = v
        g_ref[...] = gates

    row = pl.BlockSpec((tt, D), lambda i: (i, 0))
    vec = pl.BlockSpec((1, LANES), lambda i: (0, 0))
    gspec = pl.BlockSpec((tt, LANES), lambda i: (i, 0))
    return pl.pallas_call(
        body, name=name, grid=(T // tt,),
        in_specs=[main, halo, ab, pl.BlockSpec((K, 3 * D), lambda i: (0, 0)), vec, vec], out_specs=[row, row, row, gspec],
        out_shape=[jax.ShapeDtypeStruct((T, D), F32)] * 3 + [jax.ShapeDtypeStruct((T, LANES), F32)],
        compiler_params=_params("parallel"),
    )(proj, proj, proj, conv_w, alog, dtb)


def _gdn_pre_bwd(proj, conv_w, alog, dtb, dq, dk, dv, dgates, H, name):
    T = proj.shape[0]
    D = H * HEAD_DIM
    tt, main, halo, ab = _gdn_pre_specs(T, D)
    K = conv_w.shape[0]

    def body(x_ref, xh_ref, ab_ref, w_ref, al_ref, dt_ref, dq_ref, dk_ref, dv_ref, dg_ref, dc_ref, dab_ref, dal_ref, ddt_ref):
        c = _conv_tile(jnp.where(pl.program_id(0) == 0, 0.0, xh_ref[...]), x_ref[...], w_ref[...])
        _, vjp = jax.vjp(functools.partial(_gdn_act, H=H), c, ab_ref[...], al_ref[...], dt_ref[...])
        dc, dab, dal, ddt = vjp((dq_ref[...], dk_ref[...], dv_ref[...], dg_ref[...]))
        dc_ref[...] = dc
        dab_ref[...] = dab.astype(BF16)

        @pl.when(pl.program_id(0) == 0)
        def _():
            dal_ref[...] = jnp.zeros_like(dal_ref)
            ddt_ref[...] = jnp.zeros_like(ddt_ref)

        dal_ref[...] += dal
        ddt_ref[...] += ddt

    row = pl.BlockSpec((tt, D), lambda i: (i, 0))
    vec = pl.BlockSpec((1, LANES), lambda i: (0, 0))
    gspec = pl.BlockSpec((tt, LANES), lambda i: (i, 0))
    return pl.pallas_call(
        body, name=name, grid=(T // tt,),
        in_specs=[main, halo, ab, pl.BlockSpec((K, 3 * D), lambda i: (0, 0)), vec, vec, row, row, row, gspec],
        out_specs=[main, gspec, vec, vec],
        out_shape=[jax.ShapeDtypeStruct((T, 3 * D), F32), jax.ShapeDtypeStruct((T, LANES), BF16),
                   jax.ShapeDtypeStruct((1, LANES), F32), jax.ShapeDtypeStruct((1, LANES), F32)],
        compiler_params=_params("arbitrary"),
    )(proj, proj, proj, conv_w, alog, dtb, dq, dk, dv, dgates)


def _chunk_masks():
    C = GDN_CHUNK
    i = lax.broadcasted_iota(jnp.int32, (C, C), 0)
    j = lax.broadcasted_iota(jnp.int32, (C, C), 1)
    levels = []
    for lv in range(int(math.log2(C))):
        levels.append((lax.shift_right_logical(i, lv + 1) == lax.shift_right_logical(j, lv + 1))
                      & ((lax.shift_right_logical(i, lv) & 1) == 1) & ((lax.shift_right_logical(j, lv) & 1) == 0))
    return dict(tri=i >= j, strict=i > j, eye=i == j, levels=levels,
                lane=lax.broadcasted_iota(jnp.int32, (C, LANES), 1), last_row=lax.broadcasted_iota(jnp.int32, (C, 1), 0) == C - 1)


@jax.custom_vjp
def _unit_lower_inverse(L):
    m = _chunk_masks()
    A = m["eye"].astype(F32)
    for lvl in m["levels"]:
        A = A - _raw_dot("nn", X3, _raw_dot("nn", X3, A, jnp.where(lvl, L, 0.0)), A)
    return A


def _unit_lower_inverse_fwd(L):
    A = _unit_lower_inverse(L)
    return A, A


def _unit_lower_inverse_bwd(A, ct):
    return (-_raw_dot("nt", X3, _raw_dot("tn", X3, A, ct), A),)


_unit_lower_inverse.defvjp(_unit_lower_inverse_fwd, _unit_lower_inverse_bwd)


def _gdn_chunks(q, k, v, gates, S, H, m):
    C = GDN_CHUNK
    tri_f = m["tri"].astype(F32)
    eye_f = m["eye"].astype(F32)
    gam_all = _dot("nn", LHS01, tri_f, gates)
    outs, states = [], []
    for h in range(H):
        sl = slice(h * HEAD_DIM, (h + 1) * HEAD_DIM)
        qh, kh, vh, Sh = q[:, sl], k[:, sl], v[:, sl], S[:, sl]
        gam = jnp.sum(jnp.where(m["lane"] == h, gam_all, 0.0), axis=1, keepdims=True)
        beta = jnp.sum(jnp.where(m["lane"] == H + h, gates, 0.0), axis=1, keepdims=True)
        gam_row = _dot("nn", LHS01, jnp.ones((C, C), F32), jnp.where(m["eye"], gam, 0.0))
        decay = jnp.where(m["tri"], jnp.exp(jnp.where(m["tri"], gam - gam_row, 0.0)), 0.0)
        kb = kh * beta
        L = jnp.where(m["strict"], _dot("nt", None, kb, kh) * decay, 0.0)
        A = _unit_lower_inverse(L)
        eg = jnp.exp(gam)
        u = _dot("nn", X3, A, vh * beta)
        w = _dot("nn", X3, A, kb * eg)
        qk = jnp.where(m["tri"], _dot("nt", None, qh, kh) * decay, 0.0)
        g_last = jnp.sum(jnp.where(m["last_row"], gam, 0.0), axis=0, keepdims=True)
        v_new = u - _dot("nn", None, w, Sh)
        outs.append(_dot("nn", None, qh * eg, Sh) + _dot("nn", None, qk, v_new))
        states.append(Sh * jnp.exp(g_last) + _dot("tn", None, kh * jnp.exp(g_last - gam), v_new))
    return jnp.concatenate(outs, axis=1), jnp.concatenate(states, axis=1)


def _gdn_rule_fwd(q, k, v, gates, H, name):
    T, D = q.shape
    C = GDN_CHUNK

    def body(q_ref, k_ref, v_ref, g_ref, o_ref, sh_ref, s_ref):
        @pl.when(pl.program_id(0) == 0)
        def _():
            s_ref[...] = jnp.zeros_like(s_ref)

        S = s_ref[...]
        sh_ref[...] = S
        o, S_new = _gdn_chunks(q_ref[...], k_ref[...], v_ref[...], g_ref[...], S, H, _chunk_masks())
        o_ref[...] = o
        s_ref[...] = S_new

    row = pl.BlockSpec((C, D), lambda n: (n, 0))
    return pl.pallas_call(
        body, name=name, grid=(T // C,), in_specs=[row, row, row, pl.BlockSpec((C, LANES), lambda n: (n, 0))],
        out_specs=[row, pl.BlockSpec((None, HEAD_DIM, D), lambda n: (n, 0, 0))],
        out_shape=[jax.ShapeDtypeStruct((T, D), F32), jax.ShapeDtypeStruct((T // C, HEAD_DIM, D), F32)],
        scratch_shapes=[pltpu.VMEM((HEAD_DIM, D), F32)], compiler_params=_params("arbitrary"),
    )(q, k, v, gates)


def _gdn_rule_bwd(q, k, v, gates, s_hist, do, H, name):
    T, D = q.shape
    C = GDN_CHUNK
    N = T // C

    def body(q_ref, k_ref, v_ref, g_ref, s_ref, do_ref, dq_ref, dk_ref, dv_ref, dg_ref, ds_ref):
        @pl.when(pl.program_id(0) == 0)
        def _():
            ds_ref[...] = jnp.zeros_like(ds_ref)

        fn = functools.partial(_gdn_chunks, H=H, m=_chunk_masks())
        _, vjp = jax.vjp(fn, q_ref[...], k_ref[...], v_ref[...], g_ref[...], s_ref[...])
        dq, dk, dv, dg, ds = vjp((do_ref[...], ds_ref[...]))
        dq_ref[...] = dq
        dk_ref[...] = dk
        dv_ref[...] = dv
        dg_ref[...] = dg
        ds_ref[...] = ds

    row = pl.BlockSpec((C, D), lambda n: (N - 1 - n, 0))
    gspec = pl.BlockSpec((C, LANES), lambda n: (N - 1 - n, 0))
    return pl.pallas_call(
        body, name=name, grid=(N,),
        in_specs=[row, row, row, gspec, pl.BlockSpec((None, HEAD_DIM, D), lambda n: (N - 1 - n, 0, 0)), row],
        out_specs=[row, row, row, gspec],
        out_shape=[jax.ShapeDtypeStruct((T, D), F32)] * 3 + [jax.ShapeDtypeStruct((T, LANES), F32)],
        scratch_shapes=[pltpu.VMEM((HEAD_DIM, D), F32)], compiler_params=_params("arbitrary"),
    )(q, k, v, gates, s_hist, do)


def _gdn_gate(o, z, gain):
    return _headnorm(o, gain) * (z * jax.nn.sigmoid(z))


def _gdn_post_fwd(o, proj, gain, name):
    T, D = o.shape
    tt = _tile(T, 512)

    def body(o_ref, z_ref, g_ref, y_ref):
        y_ref[...] = _gdn_gate(o_ref[...], z_ref[...], g_ref[...]).astype(BF16)

    row = pl.BlockSpec((tt, D), lambda i: (i, 0))
    return pl.pallas_call(
        body, name=name, grid=(T // tt,),
        in_specs=[row, pl.BlockSpec((tt, D), lambda i: (i, 3)), pl.BlockSpec((1, HEAD_DIM), lambda i: (0, 0))], out_specs=row,
        out_shape=jax.ShapeDtypeStruct((T, D), BF16), compiler_params=_params("parallel"),
    )(o, proj, gain)


def _gdn_post_bwd(o, proj, gain, dy, name):
    T, D = o.shape
    tt = _tile(T, 512)

    def body(o_ref, z_ref, g_ref, dy_ref, do_ref, dz_ref, dg_ref):
        _, vjp = jax.vjp(_gdn_gate, o_ref[...], z_ref[...], g_ref[...])
        do, dz, dg = vjp(dy_ref[...])
        do_ref[...] = do
        dz_ref[...] = dz.astype(BF16)

        @pl.when(pl.program_id(0) == 0)
        def _():
            dg_ref[...] = jnp.zeros_like(dg_ref)

        dg_ref[...] += dg

    row = pl.BlockSpec((tt, D), lambda i: (i, 0))
    vec = pl.BlockSpec((1, HEAD_DIM), lambda i: (0, 0))
    return pl.pallas_call(
        body, name=name, grid=(T // tt,), in_specs=[row, pl.BlockSpec((tt, D), lambda i: (i, 3)), vec, row],
        out_specs=[row, row, vec],
        out_shape=[jax.ShapeDtypeStruct((T, D), F32), jax.ShapeDtypeStruct((T, D), BF16), jax.ShapeDtypeStruct((1, HEAD_DIM), F32)],
        compiler_params=_params("arbitrary"),
    )(o, proj, gain, dy)


def _scan_matmul(x, mat):
    hi, mid, lo = _split3(x)
    return (_raw_dot("nn", None, hi, mat) + _raw_dot("nn", None, mid, mat)) + _raw_dot("nn", None, lo, mat)


def _attn_tile(q, k_ref, j, row0):
    ks = k_ref[pl.ds(pl.multiple_of(j * ATTN_TK, ATTN_TK), ATTN_TK), :]
    z = _raw_dot("nt", None, q, ks) * (HEAD_DIM ** -0.5)
    row = row0 + lax.broadcasted_iota(jnp.int32, z.shape, 0)
    col = j * ATTN_TK + lax.broadcasted_iota(jnp.int32, z.shape, 1)
    causal = col < row
    l0 = -_softplus(z)
    return ks, z, causal, l0, jnp.where(causal, l0, 0.0)


def _attn_specs(T, D):
    tq = min(ATTN_TQ, T)
    qspec = pl.BlockSpec((tq, HEAD_DIM), lambda h, i: (i, h))
    kspec = pl.BlockSpec((T, HEAD_DIM), lambda h, i: (0, h))
    return tq, qspec, kspec


def _sb_attn_fwd(q, k, v, name):
    T, D = q.shape
    tq, qspec, kspec = _attn_specs(T, D)
    per = tq // ATTN_TK

    def body(q_ref, k_ref, v_ref, o_ref):
        i = pl.program_id(1)
        qv = q_ref[...]
        jj = lax.broadcasted_iota(jnp.int32, (ATTN_TK, ATTN_TK), 0)
        ss = lax.broadcasted_iota(jnp.int32, (ATTN_TK, ATTN_TK), 1)
        later = (jj > ss).astype(BF16)

        def cond(c):
            return c[3]

        def step(c):
            j, r, acc, _ = c
            _, z, causal, l0, lf = _attn_tile(qv, k_ref, j, i * tq)
            after = _scan_matmul(lf, later)
            a = jnp.where(causal, jnp.exp(z + l0 + after + r), 0.0)
            vs = v_ref[pl.ds(pl.multiple_of(j * ATTN_TK, ATTN_TK), ATTN_TK), :]
            acc = acc + _raw_dot("nn", None, a.astype(BF16), vs)
            r = r + jnp.sum(lf, axis=1, keepdims=True)
            return j - 1, r, acc, (j > 0) & (jnp.max(r) >= -EXP_UNDERFLOW)

        init = ((i + 1) * per - 1, jnp.zeros((tq, 1), F32), jnp.zeros((tq, HEAD_DIM), F32), jnp.asarray(True))
        _, _, acc, _ = lax.while_loop(cond, step, init)
        o_ref[...] = acc.astype(BF16)

    return pl.pallas_call(
        body, name=name, grid=(D // HEAD_DIM, T // tq), in_specs=[qspec, kspec, kspec], out_specs=qspec,
        out_shape=jax.ShapeDtypeStruct((T, D), BF16), compiler_params=_params("parallel", "parallel"),
    )(q, k, v)


def _sb_attn_bwd(q, k, v, do, name):
    T, D = q.shape
    tq, qspec, kspec = _attn_specs(T, D)
    per = tq // ATTN_TK
    scale = HEAD_DIM ** -0.5

    def body(q_ref, do_ref, k_ref, v_ref, dq_ref, dk_ref, dv_ref):
        i = pl.program_id(1)

        @pl.when(i == 0)
        def _():
            dk_ref[...] = jnp.zeros_like(dk_ref)
            dv_ref[...] = jnp.zeros_like(dv_ref)

        qv = q_ref[...]
        dov = do_ref[...]
        jj = lax.broadcasted_iota(jnp.int32, (ATTN_TK, ATTN_TK), 0)
        ss = lax.broadcasted_iota(jnp.int32, (ATTN_TK, ATTN_TK), 1)
        later = (jj > ss).astype(BF16)
        earlier = (jj < ss).astype(BF16)
        nkb = (i + 1) * per

        def cond(c):
            return c[2]

        def find(c):
            j, r, _ = c
            lf = _attn_tile(qv, k_ref, j, i * tq)[4]
            r = r + jnp.sum(lf, axis=1, keepdims=True)
            return j - 1, r, (j > 0) & (jnp.max(r) >= -EXP_UNDERFLOW)

        j_end, total, _ = lax.while_loop(cond, find, (nkb - 1, jnp.zeros((tq, 1), F32), jnp.asarray(True)))

        def step(j, c):
            passed, p, dq = c
            ks, z, causal, l0, lf = _attn_tile(qv, k_ref, j, i * tq)
            rows = pl.ds(pl.multiple_of(j * ATTN_TK, ATTN_TK), ATTN_TK)
            block = jnp.sum(lf, axis=1, keepdims=True)
            after = _scan_matmul(lf, later) + (total - passed - block)
            a = jnp.where(causal, jnp.exp(z + l0 + after), 0.0)
            e = a * _raw_dot("nt", None, dov, v_ref[rows, :])
            before = _scan_matmul(e, earlier) + p
            sig = jnp.exp(z + l0)
            dz = (jnp.where(causal, e * (1.0 - sig) - sig * before, 0.0) * scale).astype(BF16)
            dq = dq + _raw_dot("nn", None, dz, ks)
            dk_ref[rows, :] += _raw_dot("tn", None, dz, qv)
            dv_ref[rows, :] += _raw_dot("tn", None, a.astype(BF16), dov)
            return passed + block, p + jnp.sum(e, axis=1, keepdims=True), dq

        zero = jnp.zeros((tq, 1), F32)
        dq_ref[...] = lax.fori_loop(j_end + 1, nkb, step, (zero, zero, jnp.zeros((tq, HEAD_DIM), F32)))[2]

    return pl.pallas_call(
        body, name=name, grid=(D // HEAD_DIM, T // tq), in_specs=[qspec, qspec, kspec, kspec],
        out_specs=[qspec, kspec, kspec], out_shape=[jax.ShapeDtypeStruct((T, D), F32)] * 3,
        compiler_params=_params("parallel", "arbitrary"),
    )(q, do, k, v)


def _exchange(send, gather, name):
    shape = send.shape if gather else send.shape[1:]

    def body(x_ref, out_ref, send_sems, recv_sems, local_sem):
        x, y, c = lax.axis_index("x"), lax.axis_index("y"), lax.axis_index("c")
        me = 4 * x + 2 * y + c

        def src(slot):
            return x_ref if gather else x_ref.at[slot]

        mine = pltpu.make_async_copy(src(me), out_ref.at[me], local_sem)
        mine.start()
        copies = []
        for kk in range(1, N_DEV):
            px = 1 - x if kk & 4 else x
            py = 1 - y if kk & 2 else y
            pc = 1 - c if kk & 1 else c
            peer = 4 * px + 2 * py + pc
            out_copy = pltpu.make_async_remote_copy(
                src_ref=src(peer), dst_ref=out_ref.at[me], send_sem=send_sems.at[kk - 1], recv_sem=recv_sems.at[kk - 1],
                device_id=(px, py, pc), device_id_type=MESH)
            in_copy = pltpu.make_async_remote_copy(
                src_ref=src(me), dst_ref=out_ref.at[peer], send_sem=send_sems.at[kk - 1], recv_sem=recv_sems.at[kk - 1],
                device_id=(px, py, pc), device_id_type=MESH)
            out_copy.start()
            copies.append((out_copy, in_copy))
        for out_copy, in_copy in copies:
            out_copy.wait_send()
            in_copy.wait_recv()
        mine.wait()

    any_spec = pl.BlockSpec(memory_space=pl.ANY)
    return pl.pallas_call(
        body, name=name, in_specs=[any_spec], out_specs=any_spec,
        out_shape=jax.ShapeDtypeStruct((N_DEV,) + tuple(shape), send.dtype),
        scratch_shapes=[pltpu.SemaphoreType.DMA((N_DEV - 1,)), pltpu.SemaphoreType.DMA((N_DEV - 1,)), pltpu.SemaphoreType.DMA],
    )(send)


def _adamw(parts, w, m, v, name):
    R, C = w.shape
    tr = _tile(R, 256)

    def body(p_ref, w_ref, m_ref, v_ref, g_ref, d_ref, nm_ref, nv_ref):
        g = p_ref[0]
        for j in range(1, N_DEV):
            g = g + p_ref[j]
        nm = ADAM_B1 * m_ref[...] + (1.0 - ADAM_B1) * g
        nv = ADAM_B2 * v_ref[...] + (1.0 - ADAM_B2) * (g * g)
        m_hat = nm / (1.0 - ADAM_B1 ** ADAM_STEP)
        v_hat = nv / (1.0 - ADAM_B2 ** ADAM_STEP)
        g_ref[...] = g
        d_ref[...] = -ADAM_LR * (m_hat / (jnp.sqrt(v_hat) + ADAM_EPS) + ADAM_WD * w_ref[...])
        nm_ref[...] = nm
        nv_ref[...] = nv

    row = pl.BlockSpec((tr, C), lambda i: (i, 0))
    return pl.pallas_call(
        body, name=name, grid=(R // tr,), in_specs=[pl.BlockSpec((N_DEV, tr, C), lambda i: (0, i, 0)), row, row, row],
        out_specs=[row] * 4, out_shape=[jax.ShapeDtypeStruct((R, C), F32)] * 4, compiler_params=_params("parallel"),
    )(parts, w, m, v)


def _pack_layout(shape, align):
    n = math.prod(shape)
    if shape[-1] % LANES and shape[-1] <= PACK_COLS and n // shape[-1] >= SUBLANES:
        used, cols = n // shape[-1], shape[-1]
    else:
        used, cols = -(-n // PACK_COLS), PACK_COLS
    return -(-used // align) * align, used, cols


def _pack(arrays, dtype, align, total_align=None, lead=0):
    pieces = []
    for a in arrays:
        head = a.shape[:lead]
        rows, used, cols = _pack_layout(a.shape[lead:], align)
        if cols == PACK_COLS:
            n = math.prod(a.shape[lead:])
            flat = jnp.pad(a.astype(dtype).reshape(head + (n,)), [(0, 0)] * lead + [(0, rows * PACK_COLS - n)])
            pieces.append(flat.reshape(head + (rows, PACK_COLS)))
        else:
            mat = a.astype(dtype).reshape(head + (used, cols))
            pieces.append(jnp.pad(mat, [(0, 0)] * lead + [(0, rows - used), (0, PACK_COLS - cols)]))
    out = jnp.concatenate(pieces, axis=lead)
    if total_align:
        out = jnp.pad(out, [(0, 0)] * lead + [(0, -out.shape[lead] % total_align), (0, 0)])
    return out


def _unpack(buf, shapes, align):
    outs, r0 = [], 0
    lead = buf.shape[:-2]
    for shp in shapes:
        rows, used, cols = _pack_layout(tuple(shp), align)
        if cols == PACK_COLS:
            n = math.prod(shp)
            piece = buf[..., r0:r0 + rows, :].reshape(lead + (rows * PACK_COLS,))[..., :n]
        else:
            piece = buf[..., r0:r0 + used, :cols]
        outs.append(piece.reshape(lead + tuple(shp)))
        r0 += rows
    return outs


def _join(blocks, axis):
    g = jnp.moveaxis(blocks, 0, axis)
    shp = list(g.shape)
    return g.reshape(shp[:axis] + [shp[axis] * shp[axis + 1]] + shp[axis + 2:])


def _split(full, axis):
    shp = list(full.shape)
    g = full.reshape(shp[:axis] + [N_DEV, shp[axis] // N_DEV] + shp[axis + 1:])
    return jnp.moveaxis(g, axis, 0)


def _pad_lanes(vec):
    return jnp.pad(vec.astype(F32), (0, LANES - vec.shape[0])).reshape(1, LANES)


def _ffn_fwd(x, norm, w_up, conv_w, w_down, tag):
    h = _rmsnorm_fwd(x, norm, f"{tag}_norm")
    u_pre = _matmul(h, w_up, "nn", F32, f"{tag}_up")
    act = _ffn_mid_fwd(u_pre, conv_w, f"{tag}_mid")
    out = _matmul(act, w_down, "nn", F32, f"{tag}_down", residual=x)
    return out, (x, h, u_pre, act)


def _ffn_bwd(saved, norm, w_up, conv_w, w_down, dx, dxb, tag):
    x, h, u_pre, act = saved
    d_w_down = _matmul(act, dxb, "tn", F32, f"{tag}_dwdown")
    dact = _matmul(dxb, w_down, "nt", F32, f"{tag}_dact")
    du = _ffn_mid_bwd(u_pre, conv_w, dact, f"{tag}_dmid")
    du_pre, d_conv = _conv_bwd(du, u_pre, conv_w, f"{tag}_dconv")
    d_w_up = _matmul(h, du_pre, "tn", F32, f"{tag}_dwup")
    dh = _matmul(du_pre, w_up, "nt", F32, f"{tag}_dh")
    dx, dxb, d_norm = _rmsnorm_bwd(x, norm, dh, dx, f"{tag}_dnorm")
    return dx, dxb, dict(norm=d_norm, w_up=d_w_up, conv=d_conv, w_down=d_w_down)


def _gdn_fwd(x, norm, w_in, conv_w, alog, dtb, gain, w_out, H, tag):
    h = _rmsnorm_fwd(x, norm, f"{tag}_norm")
    proj = _matmul(h, w_in, "nn", F32, f"{tag}_in")
    q, k, v, gates = _gdn_pre_fwd(proj, conv_w, alog, dtb, H, f"{tag}_pre")
    o, s_hist = _gdn_rule_fwd(q, k, v, gates, H, f"{tag}_rule")
    y = _gdn_post_fwd(o, proj, gain, f"{tag}_post")
    out = _matmul(y, w_out, "nn", F32, f"{tag}_out", residual=x)
    return out, (x, h, proj, q, k, v, gates, o, s_hist, y)


def _gdn_bwd(saved, norm, w_in, conv_w, alog, dtb, gain, w_out, dx, dxb, H, tag):
    x, h, proj, q, k, v, gates, o, s_hist, y = saved
    D = H * HEAD_DIM
    d_w_out = _matmul(y, dxb, "tn", F32, f"{tag}_dwout")
    dy = _matmul(dxb, w_out, "nt", F32, f"{tag}_dy")
    do, dz, d_gain = _gdn_post_bwd(o, proj, gain, dy, f"{tag}_dpost")
    dq, dk, dv, dgates = _gdn_rule_bwd(q, k, v, gates, s_hist, do, H, f"{tag}_drule")
    dc, dab, d_alog, d_dtb = _gdn_pre_bwd(proj, conv_w, alog, dtb, dq, dk, dv, dgates, H, f"{tag}_dpre")
    dqkv, d_conv = _conv_bwd(dc, proj, conv_w, f"{tag}_dconv")
    dproj = jnp.concatenate([dqkv, dz, dab], axis=1)
    d_w_in = _matmul(h, dproj, "tn", F32, f"{tag}_dwin")
    dh = _matmul(dproj, w_in, "nt", F32, f"{tag}_dh")
    dx, dxb, d_norm = _rmsnorm_bwd(x, norm, dh, dx, f"{tag}_dnorm")
    grads = dict(norm=d_norm, w_in=d_w_in[:, :4 * D + 2 * H], conv=d_conv, alog=d_alog[0, :H], dtb=d_dtb[0, :H],
                 gain=d_gain[0], w_out=d_w_out)
    return dx, dxb, grads


def _sb_fwd(x, norm, w_q, q_gain, w_out, k, v, tag):
    h = _rmsnorm_fwd(x, norm, f"{tag}_norm")
    qp = _matmul(h, w_q, "nn", F32, f"{tag}_q")
    q = _qnorm_fwd(qp, q_gain, f"{tag}_qnorm")
    o = _sb_attn_fwd(q, k, v, f"{tag}_attn")
    out = _matmul(o, w_out, "nn", F32, f"{tag}_out", residual=x)
    return out, (x, h, qp, q, o)


def _sb_bwd(saved, norm, w_q, q_gain, w_out, k, v, dx, dxb, tag):
    x, h, qp, q, o = saved
    d_w_out = _matmul(o, dxb, "tn", F32, f"{tag}_dwout")
    do = _matmul(dxb, w_out, "nt", BF16, f"{tag}_do")
    dq, dk, dv = _sb_attn_bwd(q, k, v, do, f"{tag}_dattn")
    dqp, d_qgain = _qnorm_bwd(qp, q_gain, dq, f"{tag}_dqnorm")
    d_w_q = _matmul(h, dqp, "tn", F32, f"{tag}_dwq")
    dh = _matmul(dqp, w_q, "nt", F32, f"{tag}_dh")
    dx, dxb, d_norm = _rmsnorm_bwd(x, norm, dh, dx, f"{tag}_dnorm")
    return dx, dxb, dk, dv, dict(norm=d_norm, w_q=d_w_q, q_gain=d_qgain[0], w_out=d_w_out)


def _local_step(x, target, W):
    T, D = x.shape
    H = D // HEAD_DIM
    n_a, n_b = W["a_w_in"].shape[0], W["b_w_q"].shape[0]
    row = lambda a: a.reshape(1, -1)
    saved = []
    for l in range(n_a):
        x, s = _gdn_fwd(x, row(W["a_norm"][l]), W["a_w_in"][l], W["a_conv"][l], _pad_lanes(W["a_log"][l]),
                        _pad_lanes(W["a_dt_bias"][l]), row(W["a_out_norm"][l]), W["a_w_out"][l], H, f"gdn{l}")
        saved.append(s)
        x, s = _ffn_fwd(x, row(W["ffn_norm"][l]), W["ffn_w_up"][l], W["ffn_conv"][l], W["ffn_w_down"][l], f"ffn{l}")
        saved.append(s)
    x_kv = x
    h_kv = _rmsnorm_fwd(x_kv, row(W["kv_norm"]), "kv_norm")
    kv = _matmul(h_kv, W["w_kv"], "nn", F32, "kv_proj")
    k, v = _kv_post_fwd(kv, row(W["k_norm"]), "kv_post")
    for j in range(n_b):
        l = n_a + j
        x, s = _sb_fwd(x, row(W["b_norm"][j]), W["b_w_q"][j], row(W["q_norm"][j]), W["b_w_out"][j], k, v, f"sb{j}")
        saved.append(s)
        x, s = _ffn_fwd(x, row(W["ffn_norm"][l]), W["ffn_w_up"][l], W["ffn_conv"][l], W["ffn_w_down"][l], f"ffn{l}")
        saved.append(s)

    loss, dx, dxb = _loss_grad(x, target, "loss")

    G = {name: [None] * W[name].shape[0] for name in
         ("a_norm", "a_w_in", "a_conv", "a_log", "a_dt_bias", "a_out_norm", "a_w_out", "b_norm", "b_w_q", "q_norm", "b_w_out",
          "ffn_norm", "ffn_w_up", "ffn_conv", "ffn_w_down")}

    def take_ffn(l, g):
        G["ffn_norm"][l], G["ffn_w_up"][l], G["ffn_conv"][l], G["ffn_w_down"][l] = g["norm"][0], g["w_up"], g["conv"], g["w_down"]

    dks, dvs = [], []
    for j in reversed(range(n_b)):
        l = n_a + j
        dx, dxb, g = _ffn_bwd(saved.pop(), row(W["ffn_norm"][l]), W["ffn_w_up"][l], W["ffn_conv"][l], W["ffn_w_down"][l],
                              dx, dxb, f"ffn{l}")
        take_ffn(l, g)
        dx, dxb, dk, dv, g = _sb_bwd(saved.pop(), row(W["b_norm"][j]), W["b_w_q"][j], row(W["q_norm"][j]), W["b_w_out"][j],
                                     k, v, dx, dxb, f"sb{j}")
        G["b_norm"][j], G["b_w_q"][j], G["q_norm"][j], G["b_w_out"][j] = g["norm"][0], g["w_q"], g["q_gain"], g["w_out"]
        dks.append(dk)
        dvs.append(dv)
    assert n_b == 2
    dkv, d_k_norm = _kv_post_bwd(kv, row(W["k_norm"]), dks[0], dks[1], dvs[0], dvs[1], "kv_dpost")
    G["w_kv"] = _matmul(h_kv, dkv, "tn", F32, "kv_dw")
    dh_kv = _matmul(dkv, W["w_kv"], "nt", F32, "kv_dh")
    dx, dxb, d_kv_norm = _rmsnorm_bwd(x_kv, row(W["kv_norm"]), dh_kv, dx, "kv_dnorm")
    G["k_norm"], G["kv_norm"] = d_k_norm[0], d_kv_norm[0]
    for l in reversed(range(n_a)):
        dx, dxb, g = _ffn_bwd(saved.pop(), row(W["ffn_norm"][l]), W["ffn_w_up"][l], W["ffn_conv"][l], W["ffn_w_down"][l],
                              dx, dxb, f"ffn{l}")
        take_ffn(l, g)
        dx, dxb, g = _gdn_bwd(saved.pop(), row(W["a_norm"][l]), W["a_w_in"][l], W["a_conv"][l], _pad_lanes(W["a_log"][l]),
                              _pad_lanes(W["a_dt_bias"][l]), row(W["a_out_norm"][l]), W["a_w_out"][l], dx, dxb, H, f"gdn{l}")
        (G["a_norm"][l], G["a_w_in"][l], G["a_conv"][l], G["a_log"][l], G["a_dt_bias"][l], G["a_out_norm"][l],
         G["a_w_out"][l]) = g["norm"][0], g["w_in"], g["conv"], g["alog"], g["dtb"], g["gain"], g["w_out"]
    grads = {name: (jnp.stack(val) if isinstance(val, list) else val) for name, val in G.items()}
    return loss, dx, grads


WEIGHTS = ["a_norm", "a_w_in", "a_conv", "a_log", "a_dt_bias", "a_out_norm", "a_w_out", "kv_norm", "w_kv", "k_norm", "b_norm",
           "b_w_q", "q_norm", "b_w_out", "ffn_norm", "ffn_w_up", "ffn_conv", "ffn_w_down"]
SHARD_AXIS = {"a_norm": 1, "a_w_in": 2, "a_conv": 2, "a_w_out": 1, "w_kv": 1, "b_w_q": 1, "b_w_out": 1, "ffn_w_up": 2,
              "ffn_conv": 2, "ffn_w_down": 1}
MATMUL_WEIGHTS = ["a_w_in", "a_w_out", "w_kv", "b_w_q", "b_w_out", "ffn_w_up", "ffn_w_down"]
VECTOR_WEIGHTS = ["a_norm", "a_conv", "ffn_conv"]
SHARDED = MATMUL_WEIGHTS + VECTOR_WEIGHTS
REPLICATED = [n for n in WEIGHTS if n not in SHARD_AXIS]
ADAM_ROW_TILE = 256


def kernel(x, a_norm, a_w_in, a_conv, a_log, a_dt_bias, a_out_norm, a_w_out, kv_norm, w_kv, k_norm, b_norm, b_w_q, q_norm, b_w_out, ffn_norm, ffn_w_up, ffn_conv, ffn_w_down, loss_target, m_a_norm, m_a_w_in, m_a_conv, m_a_log, m_a_dt_bias, m_a_out_norm, m_a_w_out, m_kv_norm, m_w_kv, m_k_norm, m_b_norm, m_b_w_q, m_q_norm, m_b_w_out, m_ffn_norm, m_ffn_w_up, m_ffn_conv, m_ffn_w_down, v_a_norm, v_a_w_in, v_a_conv, v_a_log, v_a_dt_bias, v_a_out_norm, v_a_w_out, v_kv_norm, v_w_kv, v_k_norm, v_b_norm, v_b_w_q, v_q_norm, v_b_w_out, v_ffn_norm, v_ffn_w_up, v_ffn_conv, v_ffn_w_down):
    w = dict(a_norm=a_norm, a_w_in=a_w_in, a_conv=a_conv, a_log=a_log, a_dt_bias=a_dt_bias, a_out_norm=a_out_norm, a_w_out=a_w_out,
             kv_norm=kv_norm, w_kv=w_kv, k_norm=k_norm, b_norm=b_norm, b_w_q=b_w_q, q_norm=q_norm, b_w_out=b_w_out,
             ffn_norm=ffn_norm, ffn_w_up=ffn_w_up, ffn_conv=ffn_conv, ffn_w_down=ffn_w_down)
    m = dict(a_norm=m_a_norm, a_w_in=m_a_w_in, a_conv=m_a_conv, a_log=m_a_log, a_dt_bias=m_a_dt_bias, a_out_norm=m_a_out_norm,
             a_w_out=m_a_w_out, kv_norm=m_kv_norm, w_kv=m_w_kv, k_norm=m_k_norm, b_norm=m_b_norm, b_w_q=m_b_w_q, q_norm=m_q_norm,
             b_w_out=m_b_w_out, ffn_norm=m_ffn_norm, ffn_w_up=m_ffn_w_up, ffn_conv=m_ffn_conv, ffn_w_down=m_ffn_w_down)
    v = dict(a_norm=v_a_norm, a_w_in=v_a_w_in, a_conv=v_a_conv, a_log=v_a_log, a_dt_bias=v_a_dt_bias, a_out_norm=v_a_out_norm,
             a_w_out=v_a_w_out, kv_norm=v_kv_norm, w_kv=v_w_kv, k_norm=v_k_norm, b_norm=v_b_norm, b_w_q=v_b_w_q, q_norm=v_q_norm,
             b_w_out=v_b_w_out, ffn_norm=v_ffn_norm, ffn_w_up=v_ffn_w_up, ffn_conv=v_ffn_conv, ffn_w_down=v_ffn_w_down)
    D = x.shape[-1]
    H = D // HEAD_DIM

    mat_shapes = [w[n].shape for n in MATMUL_WEIGHTS]
    vec_shapes = [w[n].shape for n in VECTOR_WEIGHTS]
    mats = _exchange(_pack([w[n] for n in MATMUL_WEIGHTS], BF16, PACK_ROW_ALIGN), True, "gather_matmul_weights")
    vecs = _exchange(_pack([w[n] for n in VECTOR_WEIGHTS], F32, SUBLANES), True, "gather_vector_weights")
    W = {n: w[n] for n in REPLICATED}
    for n, blocks in zip(MATMUL_WEIGHTS, _unpack(mats, mat_shapes, PACK_ROW_ALIGN)):
        W[n] = _join(blocks, SHARD_AXIS[n])
    for n, blocks in zip(VECTOR_WEIGHTS, _unpack(vecs, vec_shapes, SUBLANES)):
        W[n] = _join(blocks, SHARD_AXIS[n])
    W["a_w_in"] = jnp.pad(W["a_w_in"], ((0, 0), (0, 0), (0, 4 * D + LANES - W["a_w_in"].shape[2])))

    loss, grad_x, G = _local_step(x[0], loss_target[0], W)

    send = _pack([_split(G[n], SHARD_AXIS[n]) for n in SHARDED], F32, SUBLANES, ADAM_ROW_TILE, lead=1)
    parts = _exchange(send, False, "scatter_gradients")
    shard_shapes = [w[n].shape for n in SHARDED]
    pk = lambda d: _pack([d[n] for n in SHARDED], F32, SUBLANES, ADAM_ROW_TILE)
    outs = _adamw(parts, pk(w), pk(m), pk(v), "adamw_shards")
    big = [dict(zip(SHARDED, _unpack(o, shard_shapes, SUBLANES))) for o in outs]

    rep_shapes = [w[n].shape for n in REPLICATED]
    rp = lambda d: _pack([d[n] for n in REPLICATED], F32, SUBLANES, SUBLANES)
    rep_parts = _exchange(rp(G), True, "gather_replicated_gradients")
    outs = _adamw(rep_parts, rp(w), rp(m), rp(v), "adamw_replicated")
    small = [dict(zip(REPLICATED, _unpack(o, rep_shapes, SUBLANES))) for o in outs]

    total_loss = lax.psum(loss[0, 0], ("x", "y", "c"))
    result = [total_loss, grad_x[None]]
    for kind in range(4):
        result += [big[kind][n] if n in SHARD_AXIS else small[kind][n] for n in WEIGHTS]
    return tuple(result)
```
